```python
import math
import jax, jax.numpy as jnp
from jax import lax
import numpy as np

D_MODEL = 2048
BATCH = 8
SEQ = 2048
DEPTH = 2

N_MIXERS = 2
N_POOL_LAYERS = (DEPTH + 1) // 2
N_SSM_LAYERS = DEPTH // 2

ALPHA = (2.0 * DEPTH) ** 0.25
BETA = (8.0 * DEPTH) ** -0.25
LN_EPS = 1e-5

POOL_WINDOWS = (2, 4, 8, 16)
N_POOL_GROUPS = len(POOL_WINDOWS)
POOL_GROUP_DIM = D_MODEL // N_POOL_GROUPS

SSM_EXPAND = 2
D_INNER = SSM_EXPAND * D_MODEL
SSM_HEAD_DIM = 64
SSM_HEADS = D_INNER // SSM_HEAD_DIM
SSM_GROUPS = 8
HEADS_PER_GROUP = SSM_HEADS // SSM_GROUPS
D_STATE = 128
CONV_WIDTH = 4
CHUNK = 128
CONV_DIM = D_INNER + 2 * SSM_GROUPS * D_STATE
D_IN_PROJ = D_INNER + CONV_DIM + SSM_HEADS
RMS_EPS = 1e-5

D_FF = 4 * D_MODEL

PLE_DIM = 256

kernel_name = "pool_ssd_interleaved_deepnorm_hybrid"


def layer_norm(x, g, b):
    xf = x.astype(jnp.float32)
    mu = jnp.mean(xf, axis=-1, keepdims=True)
    var = jnp.mean(jnp.square(xf - mu), axis=-1, keepdims=True)
    y = (xf - mu) * lax.rsqrt(var + LN_EPS) * g.astype(jnp.float32) + b.astype(jnp.float32)
    return y.astype(x.dtype)


def rms_norm(x, g):
    xf = x.astype(jnp.float32)
    y = xf * lax.rsqrt(jnp.mean(jnp.square(xf), axis=-1, keepdims=True) + RMS_EPS)
    return y * g.astype(jnp.float32)


def pool_mixer(x, w, scale):
    bsz, seq, _ = x.shape
    xf = x.astype(jnp.float32)
    cs = jnp.cumsum(xf, axis=1)
    pos = jnp.arange(seq)
    outs = []
    for g, win in enumerate(POOL_WINDOWS):
        sl = slice(g * POOL_GROUP_DIM, (g + 1) * POOL_GROUP_DIM)
        c = cs[..., sl]
        c_prev = jnp.pad(c, ((0, 0), (win, 0), (0, 0)))[:, :seq]
        cnt = jnp.minimum(pos + 1, win).astype(jnp.float32)[:, None]
        outs.append((c - c_prev) / cnt - xf[..., sl])
    pooled = jnp.stack(outs, axis=2).astype(x.dtype)
    y = jnp.einsum('bsgc,gcd->bsgd', pooled, w).reshape(bsz, seq, D_MODEL)
    return y * scale


def causal_depthwise_conv(u, w, b):
    seq = u.shape[1]
    up = jnp.pad(u, ((0, 0), (CONV_WIDTH - 1, 0), (0, 0)))
    out = b
    for k in range(CONV_WIDTH):
        out = out + up[:, k:k + seq] * w[k]
    return out


def ssd_mixer(x, in_w, conv_w, conv_b, dt_bias, a_log, d_skip, norm_w, out_w):
    bsz, seq, _ = x.shape
    nc = seq // CHUNK
    zxbcdt = x @ in_w
    z = zxbcdt[..., :D_INNER]
    xbc = zxbcdt[..., D_INNER:D_INNER + CONV_DIM]
    dt = zxbcdt[..., D_INNER + CONV_DIM:]
    xbc = jax.nn.silu(causal_depthwise_conv(xbc, conv_w, conv_b))
    xs = xbc[..., :D_INNER]
    bm = xbc[..., D_INNER:D_INNER + SSM_GROUPS * D_STATE]
    cm = xbc[..., D_INNER + SSM_GROUPS * D_STATE:]

    dt = jax.nn.softplus(dt.astype(jnp.float32) + dt_bias.astype(jnp.float32))
    a = -jnp.exp(a_log.astype(jnp.float32)).reshape(SSM_GROUPS, HEADS_PER_GROUP)

    xs = xs.astype(jnp.float32).reshape(bsz, nc, CHUNK, SSM_GROUPS, HEADS_PER_GROUP, SSM_HEAD_DIM)
    bm = bm.astype(jnp.float32).reshape(bsz, nc, CHUNK, SSM_GROUPS, D_STATE)
    cm = cm.astype(jnp.float32).reshape(bsz, nc, CHUNK, SSM_GROUPS, D_STATE)
    dt = dt.reshape(bsz, nc, CHUNK, SSM_GROUPS, HEADS_PER_GROUP)

    da = jnp.transpose(dt * a, (0, 3, 4, 1, 2))
    a_cs = jnp.cumsum(da, axis=-1)
    xdt = xs * dt[..., None]

    causal = jnp.tril(jnp.ones((CHUNK, CHUNK), dtype=bool))
    seg = a_cs[..., :, None] - a_cs[..., None, :]
    lmat = jnp.exp(jnp.where(causal, seg, -jnp.inf))
    cb = jnp.einsum('bclgn,bcsgn->bgcls', cm, bm)
    mmat = cb[:, :, None] * lmat
    y_diag = jnp.einsum('bghcls,bcsghp->bclghp', mmat, xdt)

    decay_states = jnp.exp(a_cs[..., -1:] - a_cs)
    xdt_dec = xdt * jnp.transpose(decay_states, (0, 3, 4, 1, 2))[..., None]
    states = jnp.einsum('bclgn,bclghp->bcghpn', bm, xdt_dec)
    chunk_decay = jnp.exp(a_cs[..., -1])

    def step(h, inp):
        s, d = inp
        return d[..., None, None] * h + s, h

    h0 = jnp.zeros((bsz, SSM_GROUPS, HEADS_PER_GROUP, SSM_HEAD_DIM, D_STATE), jnp.float32)
    _, prev = lax.scan(step, h0, (jnp.moveaxis(states, 1, 0), jnp.moveaxis(chunk_decay, 3, 0)))
    prev = jnp.moveaxis(prev, 0, 1)

    state_decay = jnp.transpose(jnp.exp(a_cs), (0, 3, 4, 1, 2))
    y_off = jnp.einsum('bclgn,bcghpn->bclghp', cm, prev) * state_decay[..., None]

    dsk = d_skip.astype(jnp.float32).reshape(SSM_GROUPS, HEADS_PER_GROUP)[..., None]
    y = (y_diag + y_off + xs * dsk).reshape(bsz, seq, D_INNER)
    y = rms_norm(y * jax.nn.silu(z.astype(jnp.float32)), norm_w).astype(x.dtype)
    return y @ out_w


def sq_relu_mlp(x, w1, w2):
    h = jax.nn.relu(x @ w1)
    return (h * h) @ w2


def setup_inputs(seed: int = 0) -> dict:
    key = jax.random.key(seed)
    ks = iter(jax.random.split(key, 32))
    f32 = jnp.float32

    def nrm(shape, scale):
        return jax.random.normal(next(ks), shape, f32) * scale

    x = nrm((BATCH, SEQ, D_MODEL), 1.0)
    p = nrm((DEPTH, BATCH, SEQ, PLE_DIM), 1.0)

    pool_w = nrm((N_POOL_LAYERS, N_POOL_GROUPS, POOL_GROUP_DIM, POOL_GROUP_DIM), BETA * POOL_GROUP_DIM ** -0.5)
    pool_scale = 1.0 + nrm((N_POOL_LAYERS, D_MODEL), 0.1)

    ssm_in_w = nrm((N_SSM_LAYERS, D_MODEL, D_IN_PROJ), D_MODEL ** -0.5)
    ssm_conv_w = nrm((N_SSM_LAYERS, CONV_WIDTH, CONV_DIM), CONV_WIDTH ** -0.5)
    ssm_conv_b = nrm((N_SSM_LAYERS, CONV_DIM), 0.02)
    dt0 = jnp.exp(jax.random.uniform(next(ks), (N_SSM_LAYERS, SSM_HEADS), f32,
                                     math.log(1e-3), math.log(1e-1)))
    ssm_dt_bias = dt0 + jnp.log(-jnp.expm1(-dt0))
    ssm_a_log = jnp.log(jax.random.uniform(next(ks), (N_SSM_LAYERS, SSM_HEADS), f32, 1.0, 16.0))
    ssm_d = 1.0 + nrm((N_SSM_LAYERS, SSM_HEADS), 0.1)
    ssm_norm_w = 1.0 + nrm((N_SSM_LAYERS, D_INNER), 0.1)
    ssm_out_w = nrm((N_SSM_LAYERS, D_INNER, D_MODEL), BETA * D_INNER ** -0.5)

    mlp_w1 = nrm((DEPTH, D_MODEL, D_FF), D_MODEL ** -0.5)
    mlp_w2 = nrm((DEPTH, D_FF, D_MODEL), BETA * D_FF ** -0.5)

    ln_g = 1.0 + nrm((DEPTH, 2, D_MODEL), 0.1)
    ln_b = nrm((DEPTH, 2, D_MODEL), 0.02)

    ple_w = nrm((DEPTH, PLE_DIM, D_MODEL), PLE_DIM ** -0.5)
    ple_gate_w = nrm((DEPTH, D_MODEL, D_MODEL), D_MODEL ** -0.5)

    return {"x": x, "p": p,
            "pool_w": pool_w, "pool_scale": pool_scale,
            "ssm_in_w": ssm_in_w, "ssm_conv_w": ssm_conv_w, "ssm_conv_b": ssm_conv_b,
            "ssm_dt_bias": ssm_dt_bias, "ssm_a_log": ssm_a_log, "ssm_d": ssm_d,
            "ssm_norm_w": ssm_norm_w, "ssm_out_w": ssm_out_w,
            "mlp_w1": mlp_w1, "mlp_w2": mlp_w2,
            "ln_g": ln_g, "ln_b": ln_b,
            "ple_w": ple_w, "ple_gate_w": ple_gate_w}


def reference(x, p, pool_w, pool_scale, ssm_in_w, ssm_conv_w, ssm_conv_b,
              ssm_dt_bias, ssm_a_log, ssm_d, ssm_norm_w, ssm_out_w,
              mlp_w1, mlp_w2, ln_g, ln_b, ple_w, ple_gate_w):
    for i in range(DEPTH):
        j = i // N_MIXERS
        if i % N_MIXERS == 0:
            h = pool_mixer(x, pool_w[j], pool_scale[j])
        else:
            h = ssd_mixer(x, ssm_in_w[j], ssm_conv_w[j], ssm_conv_b[j], ssm_dt_bias[j],
                          ssm_a_log[j], ssm_d[j], ssm_norm_w[j], ssm_out_w[j])
        x = layer_norm(ALPHA * x + h, ln_g[i, 0], ln_b[i, 0])
        h = sq_relu_mlp(x, mlp_w1[i], mlp_w2[i])
        x = layer_norm(ALPHA * x + h, ln_g[i, 1], ln_b[i, 1])
        gate = jax.nn.sigmoid(x @ ple_gate_w[i])
        x = x + gate * (p[i] @ ple_w[i])
    return x
```

```python
import functools
import math

import jax
import jax.numpy as jnp
from jax import lax
from jax.experimental import pallas as pl
from jax.experimental.pallas import tpu as pltpu

F32 = jnp.float32
BF16 = jnp.bfloat16

D_MODEL = 2048
BATCH = 8
SEQ = 2048
DEPTH = 2
ALPHA = (2.0 * DEPTH) ** 0.25
LN_EPS = 1e-5

POOL_WINDOWS = (2, 4, 8, 16)
POOL_GROUP_DIM = D_MODEL // len(POOL_WINDOWS)
POOL_HALO = 16

D_INNER = 2 * D_MODEL
HEAD_DIM = 64
N_HEADS = D_INNER // HEAD_DIM
N_GROUPS = 8
HEADS_PER_GROUP = N_HEADS // N_GROUPS
D_STATE = 128
CONV_WIDTH = 4
CHUNK = 128
GROUP_INNER = D_INNER // N_GROUPS
GROUP_CONV = GROUP_INNER + 2 * D_STATE
RMS_EPS = 1e-5
PROJ_ROWS = 512
D_FF = 4 * D_MODEL
PLE_DIM = 256

LANES = 128
SUBLANES = 8
CONV_PAD = SUBLANES
VMEM_LIMIT = 56 * 1024 * 1024


def _layer_norm(y, g, b):
    mu = jnp.mean(y, axis=-1, keepdims=True)
    d = y - mu
    var = jnp.mean(d * d, axis=-1, keepdims=True)
    return d * lax.rsqrt(var + LN_EPS) * g + b


def _pool_ln_kernel(x_ref, halo_ref, w_ref, scale_ref, g_ref, b_ref, o_ref, *, ts):
    i = pl.program_id(1)
    halo_on = (i > 0).astype(F32)
    pos = lax.broadcasted_iota(jnp.int32, (ts, 1), 0) + i * ts
    for gi, win in enumerate(POOL_WINDOWS):
        sl = slice(gi * POOL_GROUP_DIM, (gi + 1) * POOL_GROUP_DIM)
        xg = x_ref[0, :, sl]
        s = jnp.concatenate([halo_ref[0, :, sl] * halo_on, xg], axis=0)
        k = 1
        while k < win:
            s = s + pltpu.roll(s, k, axis=0)
            k *= 2
        cnt = jnp.minimum(pos + 1, win).astype(F32)
        pooled = s[POOL_HALO:] / cnt - xg
        h = jnp.dot(pooled.astype(BF16), w_ref[gi], preferred_element_type=F32)
        o_ref[0, :, sl] = ALPHA * xg + h * scale_ref[:, sl]
    o_ref[0] = _layer_norm(o_ref[0], g_ref[...], b_ref[...])


def _pool_ln(x, w_bf, scale, g, b, *, ts=256):
    nt = SEQ // ts
    hb = ts // POOL_HALO
    return pl.pallas_call(
        functools.partial(_pool_ln_kernel, ts=ts),
        grid=(BATCH, nt),
        in_specs=[
            pl.BlockSpec((1, ts, D_MODEL), lambda b_, i: (b_, i, 0)),
            pl.BlockSpec((1, POOL_HALO, D_MODEL),
                         lambda b_, i: (b_, jnp.maximum(i * hb - 1, 0), 0)),
            pl.BlockSpec((len(POOL_WINDOWS), POOL_GROUP_DIM, POOL_GROUP_DIM),
                         lambda b_, i: (0, 0, 0)),
            pl.BlockSpec((1, D_MODEL), lambda b_, i: (0, 0)),
            pl.BlockSpec((1, D_MODEL), lambda b_, i: (0, 0)),
            pl.BlockSpec((1, D_MODEL), lambda b_, i: (0, 0)),
        ],
        out_specs=pl.BlockSpec((1, ts, D_MODEL), lambda b_, i: (b_, i, 0)),
        out_shape=jax.ShapeDtypeStruct((BATCH, SEQ, D_MODEL), F32),
        compiler_params=pltpu.CompilerParams(
            dimension_semantics=("parallel", "arbitrary"),
            vmem_limit_bytes=VMEM_LIMIT),
        name="pool_ln",
    )(x, x, w_bf, scale, g, b)


def _mlp_kernel(x_ref, w1_ref, w2_ref, g_ref, b_ref, p_ref, plew_ref, gatew_ref,
                o_ref, ob_ref, xb_s, *, n_chunks):
    j = pl.program_id(1)

    @pl.when(j == 0)
    def _():
        x = x_ref[...]
        xb_s[...] = x.astype(BF16)
        o_ref[...] = ALPHA * x

    h = jnp.dot(xb_s[...], w1_ref[...], preferred_element_type=F32)
    h = jnp.maximum(h, 0.0)
    h = h * h
    o_ref[...] += jnp.dot(h.astype(BF16), w2_ref[...], preferred_element_type=F32)

    @pl.when(j == n_chunks - 1)
    def _():
        x2 = _layer_norm(o_ref[...], g_ref[...], b_ref[...])
        gate = jax.nn.sigmoid(
            jnp.dot(x2.astype(BF16), gatew_ref[...], preferred_element_type=F32))
        pe = jnp.dot(p_ref[...].astype(BF16), plew_ref[...], preferred_element_type=F32)
        out = x2 + gate * pe
        o_ref[...] = out
        ob_ref[...] = out.astype(BF16)


def _mlp_ln_ple(x, w1_bf, w2_bf, g, b, p, plew_bf, gatew_bf, *, tm=512, tf=512):
    n_tok = x.shape[0]
    n_chunks = D_FF // tf
    return pl.pallas_call(
        functools.partial(_mlp_kernel, n_chunks=n_chunks),
        grid=(n_tok // tm, n_chunks),
        in_specs=[
            pl.BlockSpec((tm, D_MODEL), lambda i, j: (i, 0)),
            pl.BlockSpec((D_MODEL, tf), lambda i, j: (0, j)),
            pl.BlockSpec((tf, D_MODEL), lambda i, j: (j, 0)),
            pl.BlockSpec((1, D_MODEL), lambda i, j: (0, 0)),
            pl.BlockSpec((1, D_MODEL), lambda i, j: (0, 0)),
            pl.BlockSpec((tm, PLE_DIM), lambda i, j: (i, 0)),
            pl.BlockSpec((PLE_DIM, D_MODEL), lambda i, j: (0, 0),
                         pipeline_mode=pl.Buffered(1)),
            pl.BlockSpec((D_MODEL, D_MODEL), lambda i, j: (0, 0),
                         pipeline_mode=pl.Buffered(1)),
        ],
        out_specs=[
            pl.BlockSpec((tm, D_MODEL), lambda i, j: (i, 0)),
            pl.BlockSpec((tm, D_MODEL), lambda i, j: (i, 0)),
        ],
        out_shape=[
            jax.ShapeDtypeStruct((n_tok, D_MODEL), F32),
            jax.ShapeDtypeStruct((n_tok, D_MODEL), BF16),
        ],
        scratch_shapes=[pltpu.VMEM((tm, D_MODEL), BF16)],
        compiler_params=pltpu.CompilerParams(
            dimension_semantics=("parallel", "arbitrary"),
            vmem_limit_bytes=VMEM_LIMIT),
        name="mlp_ln_ple",
    )(x, w1_bf, w2_bf, g, b, p, plew_bf, gatew_bf)


def _ssd_kernel(xb_ref, wz_ref, wxbc_ref, wdt_ref, cw_ref, cb_ref, dtb_ref, alog_ref,
                dsk_ref, nw_ref, yg_ref, ssq_ref, z_s, xbc_s, dt_s, acs_s, state_s):
    g = pl.program_id(1)

    @pl.when(g == 0)
    def _():
        dt_raw = jnp.dot(xb_ref[0], wdt_ref[...], preferred_element_type=F32) + dtb_ref[...]
        dt = jnp.maximum(dt_raw, 0.0) + jnp.log1p(jnp.exp(-jnp.abs(dt_raw)))
        dt_s[...] = dt
        s = dt * (-jnp.exp(alog_ref[...]))
        in_chunk = lax.broadcasted_iota(jnp.int32, (SEQ, 1), 0) & (CHUNK - 1)
        k = 1
        while k < CHUNK:
            s = s + jnp.where(in_chunk >= k, pltpu.roll(s, k, axis=0), 0.0)
            k *= 2
        acs_s[...] = s
        ssq_ref[...] = jnp.zeros_like(ssq_ref)

    xbc_s[0:CONV_PAD, :] = jnp.zeros((CONV_PAD, GROUP_CONV), F32)
    for rb in range(SEQ // PROJ_ROWS):
        xr = xb_ref[0, rb * PROJ_ROWS:(rb + 1) * PROJ_ROWS, :]
        z_s[rb * PROJ_ROWS:(rb + 1) * PROJ_ROWS, :] = jnp.dot(
            xr, wz_ref[...], preferred_element_type=F32)
        xbc_s[CONV_PAD + rb * PROJ_ROWS:CONV_PAD + (rb + 1) * PROJ_ROWS, :] = jnp.dot(
            xr, wxbc_ref[...], preferred_element_type=F32)
    state_s[...] = jnp.zeros_like(state_s)

    cw = cw_ref[...]
    cb = cb_ref[...]
    dsk = dsk_ref[...]
    nw = nw_ref[...]
    lane = lax.broadcasted_iota(jnp.int32, (1, LANES), 1)
    lo = lane < HEAD_DIM
    ri = lax.broadcasted_iota(jnp.int32, (CHUNK, CHUNK), 0)
    ci = lax.broadcasted_iota(jnp.int32, (CHUNK, CHUNK), 1)
    causal = ri >= ci
    eye = ri == ci

    def chunk_body(c, carry):
        r0 = pl.multiple_of(c * CHUNK, CHUNK)
        rows = pl.ds(r0, CHUNK)
        ue = xbc_s[pl.ds(r0, CHUNK + CONV_PAD), :]
        acc = cb
        for k in range(CONV_WIDTH):
            off = CONV_PAD - (CONV_WIDTH - 1) + k
            acc = acc + ue[off:off + CHUNK] * cw[k:k + 1]
        u = acc * jax.nn.sigmoid(acc)
        xs = u[:, :GROUP_INNER]
        bm = u[:, GROUP_INNER:GROUP_INNER + D_STATE]
        cm = u[:, GROUP_INNER + D_STATE:]
        bm_bf = bm.astype(BF16)
        cm_bf = cm.astype(BF16)
        cbm = lax.dot_general(cm_bf, bm_bf, (((1,), (1,)), ((), ())),
                              preferred_element_type=F32)
        dtc = dt_s[rows, :]
        acs = acs_s[rows, :]
        state = state_s[...]
        y_off = jnp.dot(cm_bf, state.astype(BF16), preferred_element_type=F32)

        y_blocks, xdec_blocks, cdec_blocks = [], [], []
        for kb in range(HEADS_PER_GROUP // 2):
            bl = slice(kb * LANES, (kb + 1) * LANES)
            a_cols, d_cols = [], []
            for hh in range(2):
                sel = lane == g * HEADS_PER_GROUP + 2 * kb + hh
                a_cols.append(jnp.sum(jnp.where(sel, acs, 0.0), axis=1, keepdims=True))
                d_cols.append(jnp.sum(jnp.where(sel, dtc, 0.0), axis=1, keepdims=True))
            a_blk = jnp.where(lo, a_cols[0], a_cols[1])
            dt_blk = jnp.where(lo, d_cols[0], d_cols[1])
            a_end = a_blk[CHUNK - 1:CHUNK, :]
            xs_blk = xs[:, bl]
            xdt = xs_blk * dt_blk
            ms = []
            for hh in range(2):
                a_col = a_cols[hh]
                a_row = jnp.sum(jnp.where(eye, a_col, 0.0), axis=0, keepdims=True)
                seg = a_col - a_row
                lmat = jnp.exp(jnp.where(causal, seg, -jnp.inf))
                ms.append((cbm * lmat).astype(BF16))
            lhs = jnp.concatenate(ms, axis=1)
            rhs = jnp.concatenate([jnp.where(lo, xdt, 0.0), jnp.where(lo, 0.0, xdt)],
                                  axis=0).astype(BF16)
            y_diag = jnp.dot(lhs, rhs, preferred_element_type=F32)
            y_blocks.append(y_diag + y_off[:, bl] * jnp.exp(a_blk) + xs_blk * dsk[:, bl])
            xdec_blocks.append(xdt * jnp.exp(a_end - a_blk))
            cdec_blocks.append(jnp.exp(a_end))
        y = jnp.concatenate(y_blocks, axis=1)
        xdec = jnp.concatenate(xdec_blocks, axis=1).astype(BF16)
        cdec = jnp.concatenate(cdec_blocks, axis=1)
        st_new = jnp.dot(bm.T.astype(BF16), xdec, preferred_element_type=F32)
        state_s[...] = state * cdec + st_new

        zc = z_s[rows, :]
        v = y * (zc * jax.nn.sigmoid(zc))
        ssq_ref[0, rows, :] += jnp.sum(v * v, axis=1, keepdims=True)
        yg_ref[0, rows, :] = (v * nw).astype(BF16)
        return carry

    lax.fori_loop(0, SEQ // CHUNK, chunk_body, 0)


def _ssd_scan(xb, wz_bf, wxbc_bf, wdt_bf, cw, cb, dtb, alog, dsk, nw):
    return pl.pallas_call(
        _ssd_kernel,
        grid=(BATCH, N_GROUPS),
        in_specs=[
            pl.BlockSpec((1, SEQ, D_MODEL), lambda b_, g: (b_, 0, 0),
                         pipeline_mode=pl.Buffered(1)),
            pl.BlockSpec((D_MODEL, GROUP_INNER), lambda b_, g: (0, g)),
            pl.BlockSpec((D_MODEL, GROUP_CONV), lambda b_, g: (0, g)),
            pl.BlockSpec((D_MODEL, LANES), lambda b_, g: (0, 0)),
            pl.BlockSpec((CONV_WIDTH, GROUP_CONV), lambda b_, g: (0, g)),
            pl.BlockSpec((1, GROUP_CONV), lambda b_, g: (0, g)),
            pl.BlockSpec((1, LANES), lambda b_, g: (0, 0)),
            pl.BlockSpec((1, LANES), lambda b_, g: (0, 0)),
            pl.BlockSpec((1, GROUP_INNER), lambda b_, g: (0, g)),
            pl.BlockSpec((1, GROUP_INNER), lambda b_, g: (0, g)),
        ],
        out_specs=[
            pl.BlockSpec((1, SEQ, GROUP_INNER), lambda b_, g: (b_, 0, g)),
            pl.BlockSpec((1, SEQ, 1), lambda b_, g: (b_, 0, 0)),
        ],
        out_shape=[
            jax.ShapeDtypeStruct((BATCH, SEQ, D_INNER), BF16),
            jax.ShapeDtypeStruct((BATCH, SEQ, 1), F32),
        ],
        scratch_shapes=[
            pltpu.VMEM((SEQ, GROUP_INNER), F32),
            pltpu.VMEM((SEQ + CONV_PAD, GROUP_CONV), F32),
            pltpu.VMEM((SEQ, LANES), F32),
            pltpu.VMEM((SEQ, LANES), F32),
            pltpu.VMEM((D_STATE, GROUP_INNER), F32),
        ],
        compiler_params=pltpu.CompilerParams(
            dimension_semantics=("parallel", "arbitrary"),
            vmem_limit_bytes=VMEM_LIMIT),
        name="ssd_scan",
    )(xb, wz_bf, wxbc_bf, wdt_bf, cw, cb, dtb, alog, dsk, nw)


def _proj_ln_kernel(u_ref, w_ref, ssq_ref, x_ref, g_ref, b_ref, o_ref, *, nk):
    k = pl.program_id(1)
    part = jnp.dot(u_ref[...], w_ref[...], preferred_element_type=F32)

    @pl.when(k == 0)
    def _():
        o_ref[...] = part

    @pl.when(k > 0)
    def _():
        o_ref[...] += part

    @pl.when(k == nk - 1)
    def _():
        r = lax.rsqrt(ssq_ref[...] * (1.0 / D_INNER) + RMS_EPS)
        y = ALPHA * x_ref[...] + o_ref[...] * r
        o_ref[...] = _layer_norm(y, g_ref[...], b_ref[...])


def _proj_ln(u, w_bf, ssq, x, g, b, *, tm=512, tk=1024):
    n_tok = x.shape[0]
    nk = D_INNER // tk
    return pl.pallas_call(
        functools.partial(_proj_ln_kernel, nk=nk),
        grid=(n_tok // tm, nk),
        in_specs=[
            pl.BlockSpec((tm, tk), lambda i, k: (i, k)),
            pl.BlockSpec((tk, D_MODEL), lambda i, k: (k, 0)),
            pl.BlockSpec((tm, 1), lambda i, k: (i, 0)),
            pl.BlockSpec((tm, D_MODEL), lambda i, k: (i, 0)),
            pl.BlockSpec((1, D_MODEL), lambda i, k: (0, 0)),
            pl.BlockSpec((1, D_MODEL), lambda i, k: (0, 0)),
        ],
        out_specs=pl.BlockSpec((tm, D_MODEL), lambda i, k: (i, 0)),
        out_shape=jax.ShapeDtypeStruct((n_tok, D_MODEL), F32),
        compiler_params=pltpu.CompilerParams(
            dimension_semantics=("parallel", "arbitrary"),
            vmem_limit_bytes=VMEM_LIMIT),
        name="proj_ln",
    )(u, w_bf, ssq, x, g, b)


def _group_major_conv(a):
    lead = a.shape[:-1]
    xs = a[..., :D_INNER].reshape(lead + (N_GROUPS, GROUP_INNER))
    bs = a[..., D_INNER:D_INNER + N_GROUPS * D_STATE].reshape(lead + (N_GROUPS, D_STATE))
    cs = a[..., D_INNER + N_GROUPS * D_STATE:].reshape(lead + (N_GROUPS, D_STATE))
    return jnp.concatenate([xs, bs, cs], axis=-1).reshape(lead + (N_GROUPS * GROUP_CONV,))


def _pad_heads(a):
    pad = [(0, 0)] * (a.ndim - 1) + [(0, LANES - N_HEADS)]
    return jnp.pad(a, pad)


def kernel(x, p, pool_w, pool_scale, ssm_in_w, ssm_conv_w, ssm_conv_b, ssm_dt_bias,
           ssm_a_log, ssm_d, ssm_norm_w, ssm_out_w, mlp_w1, mlp_w2, ln_g, ln_b,
           ple_w, ple_gate_w):
    n_tok = BATCH * SEQ
    row = lambda a: a.reshape(1, -1)

    x1 = _pool_ln(x, pool_w[0].astype(BF16), row(pool_scale[0]),
                  row(ln_g[0, 0]), row(ln_b[0, 0]))
    x2, x2b = _mlp_ln_ple(
        x1.reshape(n_tok, D_MODEL), mlp_w1[0].astype(BF16), mlp_w2[0].astype(BF16),
        row(ln_g[0, 1]), row(ln_b[0, 1]), p[0].reshape(n_tok, PLE_DIM),
        ple_w[0].astype(BF16), ple_gate_w[0].astype(BF16))

    in_w = ssm_in_w[0]
    wz = in_w[:, :D_INNER].astype(BF16)
    wxbc = _group_major_conv(in_w[:, D_INNER:D_INNER + D_INNER + 2 * N_GROUPS * D_STATE]).astype(BF16)
    wdt = _pad_heads(in_w[:, D_INNER + D_INNER + 2 * N_GROUPS * D_STATE:]).astype(BF16)
    yg, ssq = _ssd_scan(
        x2b.reshape(BATCH, SEQ, D_MODEL), wz, wxbc, wdt,
        _group_major_conv(ssm_conv_w[0]), row(_group_major_conv(ssm_conv_b[0])),
        row(_pad_heads(ssm_dt_bias[0])), row(_pad_heads(ssm_a_log[0])),
        row(jnp.repeat(ssm_d[0], HEAD_DIM)), row(ssm_norm_w[0]))
    x3 = _proj_ln(yg.reshape(n_tok, D_INNER), ssm_out_w[0].astype(BF16),
                  ssq.reshape(n_tok, 1), x2, row(ln_g[1, 0]), row(ln_b[1, 0]))
    x4, _ = _mlp_ln_ple(
        x3, mlp_w1[1].astype(BF16), mlp_w2[1].astype(BF16),
        row(ln_g[1, 1]), row(ln_b[1, 1]), p[1].reshape(n_tok, PLE_DIM),
        ple_w[1].astype(BF16), ple_gate_w[1].astype(BF16))
    return x4.reshape(BATCH, SEQ, D_MODEL)
```

```python
import functools

import jax
import jax.numpy as jnp
from jax import lax
from jax.experimental import pallas as pl
from jax.experimental.pallas import tpu as pltpu

F32 = jnp.float32
BF16 = jnp.bfloat16

D_MODEL = 2048
BATCH = 8
SEQ = 2048
DEPTH = 2
ALPHA = (2.0 * DEPTH) ** 0.25
LN_EPS = 1e-5

POOL_WINDOWS = (2, 4, 8, 16)
POOL_GROUP_DIM = D_MODEL // len(POOL_WINDOWS)
POOL_HALO = 16

D_INNER = 2 * D_MODEL
HEAD_DIM = 64
N_HEADS = D_INNER // HEAD_DIM
N_GROUPS = 8
HEADS_PER_GROUP = N_HEADS // N_GROUPS
D_STATE = 128
CONV_WIDTH = 4
CHUNK = 128
GROUP_INNER = D_INNER // N_GROUPS
GROUP_CONV = GROUP_INNER + 2 * D_STATE
RMS_EPS = 1e-5
N_ITEMS = BATCH * N_GROUPS

D_FF = 4 * D_MODEL
PLE_DIM = 256

LANES = 128
SUBLANES = 8
CONV_PAD = SUBLANES
CONV_BLOCKS = GROUP_CONV // LANES
VMEM_LIMIT = 56 * 1024 * 1024


def _layer_norm(y, g, b):
    mu = jnp.mean(y, axis=-1, keepdims=True)
    d = y - mu
    var = jnp.mean(d * d, axis=-1, keepdims=True)
    return d * lax.rsqrt(var + LN_EPS) * g + b


def _pool_ln_kernel(x_ref, halo_ref, w_ref, scale_ref, g_ref, b_ref, o_ref, *, ts):
    i = pl.program_id(1)
    halo_on = (i > 0).astype(F32)
    pos = lax.broadcasted_iota(jnp.int32, (ts, 1), 0) + i * ts
    for gi, win in enumerate(POOL_WINDOWS):
        sl = slice(gi * POOL_GROUP_DIM, (gi + 1) * POOL_GROUP_DIM)
        xg = x_ref[0, :, sl]
        s = jnp.concatenate([halo_ref[0, :, sl] * halo_on, xg], axis=0)
        k = 1
        while k < win:
            s = s + pltpu.roll(s, k, axis=0)
            k *= 2
        cnt = jnp.minimum(pos + 1, win).astype(F32)
        pooled = s[POOL_HALO:] / cnt - xg
        h = jnp.dot(pooled.astype(BF16), w_ref[gi], preferred_element_type=F32)
        o_ref[0, :, sl] = ALPHA * xg + h * scale_ref[:, sl]
    o_ref[0] = _layer_norm(o_ref[0], g_ref[...], b_ref[...])


def _pool_ln(x, w_bf, scale, g, b, *, ts=256):
    nt = SEQ // ts
    hb = ts // POOL_HALO
    return pl.pallas_call(
        functools.partial(_pool_ln_kernel, ts=ts),
        grid=(BATCH, nt),
        in_specs=[
            pl.BlockSpec((1, ts, D_MODEL), lambda b_, i: (b_, i, 0)),
            pl.BlockSpec((1, POOL_HALO, D_MODEL),
                         lambda b_, i: (b_, jnp.maximum(i * hb - 1, 0), 0)),
            pl.BlockSpec((len(POOL_WINDOWS), POOL_GROUP_DIM, POOL_GROUP_DIM),
                         lambda b_, i: (0, 0, 0)),
            pl.BlockSpec((1, D_MODEL), lambda b_, i: (0, 0)),
            pl.BlockSpec((1, D_MODEL), lambda b_, i: (0, 0)),
            pl.BlockSpec((1, D_MODEL), lambda b_, i: (0, 0)),
        ],
        out_specs=pl.BlockSpec((1, ts, D_MODEL), lambda b_, i: (b_, i, 0)),
        out_shape=jax.ShapeDtypeStruct((BATCH, SEQ, D_MODEL), F32),
        compiler_params=pltpu.CompilerParams(
            dimension_semantics=("parallel", "arbitrary"),
            vmem_limit_bytes=VMEM_LIMIT),
        name="pool_ln",
    )(x, x, w_bf, scale, g, b)


def _mlp_kernel(x_ref, w1_ref, w2_ref, g_ref, b_ref, p_ref, plew_ref, gatew_ref,
                o_ref, ob_ref, xb_s, *, n_chunks):
    j = pl.program_id(1)

    @pl.when(j == 0)
    def _():
        x = x_ref[...]
        xb_s[...] = x.astype(BF16)
        o_ref[...] = ALPHA * x

    h = jnp.dot(xb_s[...], w1_ref[...], preferred_element_type=F32)
    h = jnp.maximum(h, 0.0)
    h = h * h
    o_ref[...] += jnp.dot(h.astype(BF16), w2_ref[...], preferred_element_type=F32)

    @pl.when(j == n_chunks - 1)
    def _():
        x2 = _layer_norm(o_ref[...], g_ref[...], b_ref[...])
        gate = jax.nn.sigmoid(
            jnp.dot(x2.astype(BF16), gatew_ref[...], preferred_element_type=F32))
        pe = jnp.dot(p_ref[...].astype(BF16), plew_ref[...], preferred_element_type=F32)
        out = x2 + gate * pe
        o_ref[...] = out
        ob_ref[...] = out.astype(BF16)


def _mlp_ln_ple(x, w1_bf, w2_bf, g, b, p, plew_bf, gatew_bf, *, tm=512, tf=512):
    n_tok = x.shape[0]
    n_chunks = D_FF // tf
    return pl.pallas_call(
        functools.partial(_mlp_kernel, n_chunks=n_chunks),
        grid=(n_tok // tm, n_chunks),
        in_specs=[
            pl.BlockSpec((tm, D_MODEL), lambda i, j: (i, 0)),
            pl.BlockSpec((D_MODEL, tf), lambda i, j: (0, j)),
            pl.BlockSpec((tf, D_MODEL), lambda i, j: (j, 0)),
            pl.BlockSpec((1, D_MODEL), lambda i, j: (0, 0)),
            pl.BlockSpec((1, D_MODEL), lambda i, j: (0, 0)),
            pl.BlockSpec((tm, PLE_DIM), lambda i, j: (i, 0)),
            pl.BlockSpec((PLE_DIM, D_MODEL), lambda i, j: (0, 0),
                         pipeline_mode=pl.Buffered(1)),
            pl.BlockSpec((D_MODEL, D_MODEL), lambda i, j: (0, 0),
                         pipeline_mode=pl.Buffered(1)),
        ],
        out_specs=[
            pl.BlockSpec((tm, D_MODEL), lambda i, j: (i, 0)),
            pl.BlockSpec((tm, D_MODEL), lambda i, j: (i, 0)),
        ],
        out_shape=[
            jax.ShapeDtypeStruct((n_tok, D_MODEL), F32),
            jax.ShapeDtypeStruct((n_tok, D_MODEL), BF16),
        ],
        scratch_shapes=[pltpu.VMEM((tm, D_MODEL), BF16)],
        compiler_params=pltpu.CompilerParams(
            dimension_semantics=("parallel", "arbitrary"),
            vmem_limit_bytes=VMEM_LIMIT),
        name="mlp_ln_ple",
    )(x, w1_bf, w2_bf, g, b, p, plew_bf, gatew_bf)


def _ssd_kernel(xb_ref, wz_ref, wxbc_ref, wdt_ref, cw_ref, cb_ref, dtb_ref, alog_ref,
                dsk_ref, nw_ref, yg_ref, ssq_ref, z0_s, z1_s, xbc0_s, xbc1_s, dt_s, acs_s,
                state_s):
    s = pl.program_id(0)
    t = jnp.minimum(s, N_ITEMS - 1)
    t_bslot = (t // N_GROUPS) & 1
    sc = jnp.maximum(s - 1, 0)
    g = sc % N_GROUPS
    s_bslot = (sc // N_GROUPS) & 1
    slots = ((z0_s, xbc0_s), (z1_s, xbc1_s))

    @pl.when((t % N_GROUPS == 0) & (s < N_ITEMS))
    def _():
        dt_raw = jnp.dot(xb_ref[0], wdt_ref[...], preferred_element_type=F32) + dtb_ref[...]
        dt = jnp.maximum(dt_raw, 0.0) + jnp.log1p(jnp.exp(-jnp.abs(dt_raw)))
        dt_s[t_bslot] = dt
        a = dt * (-jnp.exp(alog_ref[...]))
        in_chunk = lax.broadcasted_iota(jnp.int32, (SEQ, 1), 0) & (CHUNK - 1)
        k = 1
        while k < CHUNK:
            a = a + jnp.where(in_chunk >= k, pltpu.roll(a, k, axis=0), 0.0)
            k *= 2
        acs_s[t_bslot] = a

    @pl.when(s == 0)
    def _():
        for _, xbc_s in slots:
            for j in range(CONV_BLOCKS):
                xbc_s[j, 0:CONV_PAD, :] = jnp.zeros((CONV_PAD, LANES), F32)

    def proj_rows(c, z_s, xbc_s):
        r0 = pl.multiple_of(c * CHUNK, CHUNK)
        xr = xb_ref[0, pl.ds(r0, CHUNK), :]
        z_s[pl.ds(r0, CHUNK), :] = jnp.dot(xr, wz_ref[...], preferred_element_type=F32)
        xbc = jnp.dot(xr, wxbc_ref[...], preferred_element_type=F32)
        for j in range(CONV_BLOCKS):
            xbc_s[j, pl.ds(r0 + CONV_PAD, CHUNK), :] = xbc[:, j * LANES:(j + 1) * LANES]

    lane = lax.broadcasted_iota(jnp.int32, (1, LANES), 1)
    lo = lane < HEAD_DIM
    ri = lax.broadcasted_iota(jnp.int32, (CHUNK, CHUNK), 0)
    ci = lax.broadcasted_iota(jnp.int32, (CHUNK, CHUNK), 1)
    causal = ri >= ci
    eye = ri == ci

    def scan_chunk(c, z_s, xbc_s):
        cw = cw_ref[...]
        cb = cb_ref[...]
        dsk = dsk_ref[...]
        nw = nw_ref[...]
        r0 = pl.multiple_of(c * CHUNK, CHUNK)
        rows = pl.ds(r0, CHUNK)
        u = []
        for j in range(CONV_BLOCKS):
            bl = slice(j * LANES, (j + 1) * LANES)
            acc = cb[:, bl]
            for k in range(CONV_WIDTH):
                off = CONV_PAD - (CONV_WIDTH - 1) + k
                acc = acc + xbc_s[j, pl.ds(r0 + off, CHUNK), :] * cw[k:k + 1, bl]
            u.append(acc * jax.nn.sigmoid(acc))
        bm = u[CONV_BLOCKS - 2]
        cm = u[CONV_BLOCKS - 1]
        bm_bf = bm.astype(BF16)
        cm_bf = cm.astype(BF16)
        cbm = lax.dot_general(cm_bf, bm_bf, (((1,), (1,)), ((), ())),
                              preferred_element_type=F32)
        dtc = dt_s[s_bslot, rows, :]
        acs = acs_s[s_bslot, rows, :]
        state = state_s[...]
        y_off = jnp.dot(cm_bf, state.astype(BF16), preferred_element_type=F32)

        y_blocks, xdec_blocks, cdec_blocks = [], [], []
        for kb in range(HEADS_PER_GROUP // 2):
            bl = slice(kb * LANES, (kb + 1) * LANES)
            a_cols, d_cols = [], []
            for hh in range(2):
                sel = lane == g * HEADS_PER_GROUP + 2 * kb + hh
                a_cols.append(jnp.sum(jnp.where(sel, acs, 0.0), axis=1, keepdims=True))
                d_cols.append(jnp.sum(jnp.where(sel, dtc, 0.0), axis=1, keepdims=True))
            a_blk = jnp.where(lo, a_cols[0], a_cols[1])
            dt_blk = jnp.where(lo, d_cols[0], d_cols[1])
            a_end = a_blk[CHUNK - 1:CHUNK, :]
            xs_blk = u[kb]
            xdt = xs_blk * dt_blk
            ms = []
            for hh in range(2):
                a_col = a_cols[hh]
                a_row = jnp.sum(jnp.where(eye, a_col, 0.0), axis=0, keepdims=True)
                seg = a_col - a_row
                lmat = jnp.exp(jnp.where(causal, seg, -jnp.inf))
                ms.append((cbm * lmat).astype(BF16))
            lhs = jnp.concatenate(ms, axis=1)
            rhs = jnp.concatenate([jnp.where(lo, xdt, 0.0), jnp.where(lo, 0.0, xdt)],
                                  axis=0).astype(BF16)
            y_diag = jnp.dot(lhs, rhs, preferred_element_type=F32)
            y_blocks.append(y_diag + y_off[:, bl] * jnp.exp(a_blk) + xs_blk * dsk[:, bl])
            xdec_blocks.append(xdt * jnp.exp(a_end - a_blk))
            cdec_blocks.append(jnp.exp(a_end))
        y = jnp.concatenate(y_blocks, axis=1)
        xdec = jnp.concatenate(xdec_blocks, axis=1).astype(BF16)
        cdec = jnp.concatenate(cdec_blocks, axis=1)
        st_new = jnp.dot(bm.T.astype(BF16), xdec, preferred_element_type=F32)
        state_s[...] = state * cdec + st_new

        zc = z_s[rows, :]
        v = y * (zc * jax.nn.sigmoid(zc))
        ssq_ref[0, rows, :] += jnp.sum(v * v, axis=1, keepdims=True)
        yg_ref[0, rows, :] = (v * nw).astype(BF16)

    def run(scan_slot, proj_slot):
        def body(c, carry):
            if proj_slot is not None:
                proj_rows(c, *slots[proj_slot])
            if scan_slot is not None:
                scan_chunk(c, *slots[scan_slot])
            return carry
        lax.fori_loop(0, SEQ // CHUNK, body, 0)

    @pl.when(s == 0)
    def _():
        run(None, 0)

    @pl.when(s > 0)
    def _():
        state_s[...] = jnp.zeros_like(state_s)

        @pl.when(g == 0)
        def _():
            ssq_ref[...] = jnp.zeros_like(ssq_ref)

    for parity in range(2):
        @pl.when((s > 0) & (s < N_ITEMS) & (s % 2 == parity))
        def _():
            run(1 - parity, parity)

    @pl.when(s == N_ITEMS)
    def _():
        run((N_ITEMS - 1) % 2, None)


def _ssd_scan(xb, wz_bf, wxbc_bf, wdt_bf, cw, cb, dtb, alog, dsk, nw):
    proj_b = lambda s: jnp.minimum(s, N_ITEMS - 1) // N_GROUPS
    proj_g = lambda s: jnp.minimum(s, N_ITEMS - 1) % N_GROUPS
    scan_b = lambda s: jnp.maximum(s - 1, 0) // N_GROUPS
    scan_g = lambda s: jnp.maximum(s - 1, 0) % N_GROUPS
    return pl.pallas_call(
        _ssd_kernel,
        grid=(N_ITEMS + 1,),
        in_specs=[
            pl.BlockSpec((1, SEQ, D_MODEL), lambda s: (proj_b(s), 0, 0),
                         pipeline_mode=pl.Buffered(1)),
            pl.BlockSpec((D_MODEL, GROUP_INNER), lambda s: (0, proj_g(s))),
            pl.BlockSpec((D_MODEL, GROUP_CONV), lambda s: (0, proj_g(s))),
            pl.BlockSpec((D_MODEL, LANES), lambda s: (0, 0)),
            pl.BlockSpec((CONV_WIDTH, GROUP_CONV), lambda s: (0, scan_g(s))),
            pl.BlockSpec((1, GROUP_CONV), lambda s: (0, scan_g(s))),
            pl.BlockSpec((1, LANES), lambda s: (0, 0)),
            pl.BlockSpec((1, LANES), lambda s: (0, 0)),
            pl.BlockSpec((1, GROUP_INNER), lambda s: (0, scan_g(s))),
            pl.BlockSpec((1, GROUP_INNER), lambda s: (0, scan_g(s))),
        ],
        out_specs=[
            pl.BlockSpec((1, SEQ, GROUP_INNER), lambda s: (scan_b(s), 0, scan_g(s))),
            pl.BlockSpec((1, SEQ, 1), lambda s: (scan_b(s), 0, 0)),
        ],
        out_shape=[
            jax.ShapeDtypeStruct((BATCH, SEQ, D_INNER), BF16),
            jax.ShapeDtypeStruct((BATCH, SEQ, 1), F32),
        ],
        scratch_shapes=[
            pltpu.VMEM((SEQ, GROUP_INNER), F32),
            pltpu.VMEM((SEQ, GROUP_INNER), F32),
            pltpu.VMEM((CONV_BLOCKS, SEQ + CONV_PAD, LANES), F32),
            pltpu.VMEM((CONV_BLOCKS, SEQ + CONV_PAD, LANES), F32),
            pltpu.VMEM((2, SEQ, LANES), F32),
            pltpu.VMEM((2, SEQ, LANES), F32),
            pltpu.VMEM((D_STATE, GROUP_INNER), F32),
        ],
        compiler_params=pltpu.CompilerParams(
            dimension_semantics=("arbitrary",),
            vmem_limit_bytes=VMEM_LIMIT),
        name="ssd_scan",
    )(xb, wz_bf, wxbc_bf, wdt_bf, cw, cb, dtb, alog, dsk, nw)


def _proj_ln_kernel(u_ref, w_ref, ssq_ref, x_ref, g_ref, b_ref, o_ref):
    acc = jnp.dot(u_ref[...], w_ref[...], preferred_element_type=F32)
    r = lax.rsqrt(ssq_ref[...] * (1.0 / D_INNER) + RMS_EPS)
    y = ALPHA * x_ref[...] + acc * r
    o_ref[...] = _layer_norm(y, g_ref[...], b_ref[...])


def _proj_ln(u, w_bf, ssq, x, g, b, *, tm=512):
    n_tok = x.shape[0]
    return pl.pallas_call(
        _proj_ln_kernel,
        grid=(n_tok // tm,),
        in_specs=[
            pl.BlockSpec((tm, D_INNER), lambda i: (i, 0)),
            pl.BlockSpec((D_INNER, D_MODEL), lambda i: (0, 0), pipeline_mode=pl.Buffered(1)),
            pl.BlockSpec((tm, 1), lambda i: (i, 0)),
            pl.BlockSpec((tm, D_MODEL), lambda i: (i, 0)),
            pl.BlockSpec((1, D_MODEL), lambda i: (0, 0)),
            pl.BlockSpec((1, D_MODEL), lambda i: (0, 0)),
        ],
        out_specs=pl.BlockSpec((tm, D_MODEL), lambda i: (i, 0)),
        out_shape=jax.ShapeDtypeStruct((n_tok, D_MODEL), F32),
        compiler_params=pltpu.CompilerParams(
            dimension_semantics=("parallel",),
            vmem_limit_bytes=VMEM_LIMIT),
        name="proj_ln",
    )(u, w_bf, ssq, x, g, b)


def _group_major_conv(a):
    lead = a.shape[:-1]
    xs = a[..., :D_INNER].reshape(lead + (N_GROUPS, GROUP_INNER))
    bs = a[..., D_INNER:D_INNER + N_GROUPS * D_STATE].reshape(lead + (N_GROUPS, D_STATE))
    cs = a[..., D_INNER + N_GROUPS * D_STATE:].reshape(lead + (N_GROUPS, D_STATE))
    return jnp.concatenate([xs, bs, cs], axis=-1).reshape(lead + (N_GROUPS * GROUP_CONV,))


def _pad_heads(a):
    pad = [(0, 0)] * (a.ndim - 1) + [(0, LANES - N_HEADS)]
    return jnp.pad(a, pad)


def kernel(x, p, pool_w, pool_scale, ssm_in_w, ssm_conv_w, ssm_conv_b, ssm_dt_bias,
           ssm_a_log, ssm_d, ssm_norm_w, ssm_out_w, mlp_w1, mlp_w2, ln_g, ln_b,
           ple_w, ple_gate_w):
    n_tok = BATCH * SEQ
    row = lambda a: a.reshape(1, -1)

    x1 = _pool_ln(x, pool_w[0].astype(BF16), row(pool_scale[0]),
                  row(ln_g[0, 0]), row(ln_b[0, 0]))
    x2, x2b = _mlp_ln_ple(
        x1.reshape(n_tok, D_MODEL), mlp_w1[0].astype(BF16), mlp_w2[0].astype(BF16),
        row(ln_g[0, 1]), row(ln_b[0, 1]), p[0].reshape(n_tok, PLE_DIM),
        ple_w[0].astype(BF16), ple_gate_w[0].astype(BF16))

    in_w = ssm_in_w[0]
    wz = in_w[:, :D_INNER].astype(BF16)
    wxbc = _group_major_conv(in_w[:, D_INNER:D_INNER + D_INNER + 2 * N_GROUPS * D_STATE]).astype(BF16)
    wdt = _pad_heads(in_w[:, D_INNER + D_INNER + 2 * N_GROUPS * D_STATE:]).astype(BF16)
    yg, ssq = _ssd_scan(
        x2b.reshape(BATCH, SEQ, D_MODEL), wz, wxbc, wdt,
        _group_major_conv(ssm_conv_w[0]), row(_group_major_conv(ssm_conv_b[0])),
        row(_pad_heads(ssm_dt_bias[0])), row(_pad_heads(ssm_a_log[0])),
        row(jnp.repeat(ssm_d[0], HEAD_DIM)), row(ssm_norm_w[0]))
    x3 = _proj_ln(yg.reshape(n_tok, D_INNER), ssm_out_w[0].astype(BF16),
                  ssq.reshape(n_tok, 1), x2, row(ln_g[1, 0]), row(ln_b[1, 0]))
    x4, _ = _mlp_ln_ple(
        x3, mlp_w1[1].astype(BF16), mlp_w2[1].astype(BF16),
        row(ln_g[1, 1]), row(ln_b[1, 1]), p[1].reshape(n_tok, PLE_DIM),
        ple_w[1].astype(BF16), ple_gate_w[1].astype(BF16))
    return x4.reshape(BATCH, SEQ, D_MODEL)
```

```python
import functools

import jax
import jax.numpy as jnp
from jax import lax
from jax.experimental import pallas as pl
from jax.experimental.pallas import tpu as pltpu

F32 = jnp.float32
BF16 = jnp.bfloat16

D_MODEL = 2048
BATCH = 8
SEQ = 2048
DEPTH = 2
ALPHA = (2.0 * DEPTH) ** 0.25
LN_EPS = 1e-5

POOL_WINDOWS = (2, 4, 8, 16)
POOL_GROUP_DIM = D_MODEL // len(POOL_WINDOWS)
POOL_HALO = 16

D_INNER = 2 * D_MODEL
HEAD_DIM = 64
N_HEADS = D_INNER // HEAD_DIM
N_GROUPS = 8
HEADS_PER_GROUP = N_HEADS // N_GROUPS
D_STATE = 128
CONV_WIDTH = 4
CHUNK = 128
GROUP_INNER = D_INNER // N_GROUPS
GROUP_CONV = GROUP_INNER + 2 * D_STATE
RMS_EPS = 1e-5
N_ITEMS = BATCH * N_GROUPS

D_FF = 4 * D_MODEL
PLE_DIM = 256

LANES = 128
SUBLANES = 8
CONV_PAD = SUBLANES
CONV_BLOCKS = GROUP_CONV // LANES
PROJ_COLS = 256
CHUNKS_PER_ITER = 2
PROJ_ROWS = CHUNKS_PER_ITER * CHUNK
VMEM_LIMIT = 60 * 1024 * 1024


def _silu(x):
    h = 0.5 * x
    return h + h * jnp.tanh(h)


def _layer_norm(y, g, b):
    mu = jnp.mean(y, axis=-1, keepdims=True)
    d = y - mu
    var = jnp.mean(d * d, axis=-1, keepdims=True)
    return d * lax.rsqrt(var + LN_EPS) * g + b


def _pool_ln_kernel(x_ref, halo_ref, w_ref, scale_ref, g_ref, b_ref, o_ref, *, ts):
    i = pl.program_id(1)
    halo_on = (i > 0).astype(F32)
    pos = lax.broadcasted_iota(jnp.int32, (ts, 1), 0) + i * ts
    for gi, win in enumerate(POOL_WINDOWS):
        sl = slice(gi * POOL_GROUP_DIM, (gi + 1) * POOL_GROUP_DIM)
        xg = x_ref[0, :, sl]
        s = jnp.concatenate([halo_ref[0, :, sl] * halo_on, xg], axis=0)
        k = 1
        while k < win:
            s = s + pltpu.roll(s, k, axis=0)
            k *= 2
        cnt = jnp.minimum(pos + 1, win).astype(F32)
        pooled = s[POOL_HALO:] / cnt - xg
        h = jnp.dot(pooled.astype(BF16), w_ref[gi], preferred_element_type=F32)
        o_ref[0, :, sl] = ALPHA * xg + h * scale_ref[:, sl]
    o_ref[0] = _layer_norm(o_ref[0], g_ref[...], b_ref[...])


def _pool_ln(x, w_bf, scale, g, b, *, ts=256):
    nt = SEQ // ts
    hb = ts // POOL_HALO
    return pl.pallas_call(
        functools.partial(_pool_ln_kernel, ts=ts),
        grid=(BATCH, nt),
        in_specs=[
            pl.BlockSpec((1, ts, D_MODEL), lambda b_, i: (b_, i, 0)),
            pl.BlockSpec((1, POOL_HALO, D_MODEL),
                         lambda b_, i: (b_, jnp.maximum(i * hb - 1, 0), 0)),
            pl.BlockSpec((len(POOL_WINDOWS), POOL_GROUP_DIM, POOL_GROUP_DIM),
                         lambda b_, i: (0, 0, 0)),
            pl.BlockSpec((1, D_MODEL), lambda b_, i: (0, 0)),
            pl.BlockSpec((1, D_MODEL), lambda b_, i: (0, 0)),
            pl.BlockSpec((1, D_MODEL), lambda b_, i: (0, 0)),
        ],
        out_specs=pl.BlockSpec((1, ts, D_MODEL), lambda b_, i: (b_, i, 0)),
        out_shape=jax.ShapeDtypeStruct((BATCH, SEQ, D_MODEL), F32),
        compiler_params=pltpu.CompilerParams(
            dimension_semantics=("parallel", "arbitrary"),
            vmem_limit_bytes=VMEM_LIMIT),
        name="pool_ln",
    )(x, x, w_bf, scale, g, b)


def _mlp_kernel(x_ref, w1_ref, w2_ref, g_ref, b_ref, p_ref, plew_ref, gatew_ref,
                o_ref, ob_ref, xb_s, *, n_chunks):
    j = pl.program_id(1)

    @pl.when(j == 0)
    def _():
        x = x_ref[...]
        xb_s[...] = x.astype(BF16)
        o_ref[...] = ALPHA * x

    h = jnp.dot(xb_s[...], w1_ref[...], preferred_element_type=F32)
    h = jnp.maximum(h, 0.0)
    h = h * h
    o_ref[...] += jnp.dot(h.astype(BF16), w2_ref[...], preferred_element_type=F32)

    @pl.when(j == n_chunks - 1)
    def _():
        x2 = _layer_norm(o_ref[...], g_ref[...], b_ref[...])
        gate = jax.nn.sigmoid(
            jnp.dot(x2.astype(BF16), gatew_ref[...], preferred_element_type=F32))
        pe = jnp.dot(p_ref[...].astype(BF16), plew_ref[...], preferred_element_type=F32)
        out = x2 + gate * pe
        o_ref[...] = out
        ob_ref[...] = out.astype(BF16)


def _mlp_ln_ple(x, w1_bf, w2_bf, g, b, p, plew_bf, gatew_bf, *, tm=512, tf=1024):
    n_tok = x.shape[0]
    n_chunks = D_FF // tf
    return pl.pallas_call(
        functools.partial(_mlp_kernel, n_chunks=n_chunks),
        grid=(n_tok // tm, n_chunks),
        in_specs=[
            pl.BlockSpec((tm, D_MODEL), lambda i, j: (i, 0)),
            pl.BlockSpec((D_MODEL, tf), lambda i, j: (0, j)),
            pl.BlockSpec((tf, D_MODEL), lambda i, j: (j, 0)),
            pl.BlockSpec((1, D_MODEL), lambda i, j: (0, 0)),
            pl.BlockSpec((1, D_MODEL), lambda i, j: (0, 0)),
            pl.BlockSpec((tm, PLE_DIM), lambda i, j: (i, 0)),
            pl.BlockSpec((PLE_DIM, D_MODEL), lambda i, j: (0, 0),
                         pipeline_mode=pl.Buffered(1)),
            pl.BlockSpec((D_MODEL, D_MODEL), lambda i, j: (0, 0),
                         pipeline_mode=pl.Buffered(1)),
        ],
        out_specs=[
            pl.BlockSpec((tm, D_MODEL), lambda i, j: (i, 0)),
            pl.BlockSpec((tm, D_MODEL), lambda i, j: (i, 0)),
        ],
        out_shape=[
            jax.ShapeDtypeStruct((n_tok, D_MODEL), F32),
            jax.ShapeDtypeStruct((n_tok, D_MODEL), BF16),
        ],
        scratch_shapes=[pltpu.VMEM((tm, D_MODEL), BF16)],
        compiler_params=pltpu.CompilerParams(
            dimension_semantics=("parallel", "arbitrary"),
            vmem_limit_bytes=VMEM_LIMIT),
        name="mlp_ln_ple",
    )(x, w1_bf, w2_bf, g, b, p, plew_bf, gatew_bf)


def _ssd_kernel(xb_ref, wz_ref, wxbc_ref, wdt_ref, cw_ref, cb_ref, dtb_ref, alog_ref,
                dsk_ref, nw_ref, yg_ref, ssq_ref, z0_s, z1_s, xbc0_s, xbc1_s, dt_s, acs_s,
                acst_s, state_s):
    s = pl.program_id(0)
    t = jnp.minimum(s, N_ITEMS - 1)
    t_bslot = (t // N_GROUPS) & 1
    sc = jnp.maximum(s - 1, 0)
    g = sc % N_GROUPS
    s_bslot = (sc // N_GROUPS) & 1
    slots = ((z0_s, xbc0_s), (z1_s, xbc1_s))

    @pl.when((t % N_GROUPS == 0) & (s < N_ITEMS))
    def _():
        dt_raw = jnp.dot(xb_ref[0], wdt_ref[...], preferred_element_type=F32) + dtb_ref[...]
        dt = jnp.maximum(dt_raw, 0.0) + jnp.log1p(jnp.exp(-jnp.abs(dt_raw)))
        dt_s[t_bslot] = dt
        a = dt * (-jnp.exp(alog_ref[...]))
        in_chunk = lax.broadcasted_iota(jnp.int32, (SEQ, 1), 0) & (CHUNK - 1)
        k = 1
        while k < CHUNK:
            a = a + jnp.where(in_chunk >= k, pltpu.roll(a, k, axis=0), 0.0)
            k *= 2
        acs_s[t_bslot] = a
        for c in range(SEQ // CHUNK):
            acst_s[t_bslot, c] = a[c * CHUNK:(c + 1) * CHUNK, :].T

    @pl.when(s == 0)
    def _():
        for _, xbc_s in slots:
            for j in range(CONV_BLOCKS):
                xbc_s[j, 0:CONV_PAD, :] = jnp.zeros((CONV_PAD, LANES), F32)

    def proj_pieces(c, z_s, xbc_s):
        r0 = pl.multiple_of(c * PROJ_ROWS, PROJ_ROWS)
        xr = xb_ref[0, pl.ds(r0, PROJ_ROWS), :]

        def z_piece(n):
            cols = slice(n * PROJ_COLS, (n + 1) * PROJ_COLS)
            z_s[pl.ds(r0, PROJ_ROWS), cols] = jnp.dot(
                xr, wz_ref[:, cols], preferred_element_type=F32)

        def xbc_piece(n):
            cols = slice(n * PROJ_COLS, (n + 1) * PROJ_COLS)
            res = jnp.dot(xr, wxbc_ref[:, cols], preferred_element_type=F32)
            for jj in range(PROJ_COLS // LANES):
                j = n * (PROJ_COLS // LANES) + jj
                xbc_s[j, pl.ds(r0 + CONV_PAD, PROJ_ROWS), :] = res[:, jj * LANES:(jj + 1) * LANES]

        return ([functools.partial(z_piece, n) for n in range(GROUP_INNER // PROJ_COLS)]
                + [functools.partial(xbc_piece, n) for n in range(GROUP_CONV // PROJ_COLS)])

    lane = lax.broadcasted_iota(jnp.int32, (1, LANES), 1)
    lo = lane < HEAD_DIM
    ri = lax.broadcasted_iota(jnp.int32, (CHUNK, CHUNK), 0)
    ci = lax.broadcasted_iota(jnp.int32, (CHUNK, CHUNK), 1)
    causal = ri >= ci

    def scan_chunk(c, z_s, xbc_s, fill):
        cw = cw_ref[...]
        cb = cb_ref[...]
        dsk = dsk_ref[...]
        nw = nw_ref[...]
        r0 = pl.multiple_of(c * CHUNK, CHUNK)
        rows = pl.ds(r0, CHUNK)
        u = []
        for j in range(CONV_BLOCKS):
            bl = slice(j * LANES, (j + 1) * LANES)
            acc = cb[:, bl]
            for k in range(CONV_WIDTH):
                off = CONV_PAD - (CONV_WIDTH - 1) + k
                acc = acc + xbc_s[j, pl.ds(r0 + off, CHUNK), :] * cw[k:k + 1, bl]
            u.append(_silu(acc))
            if j % 2 == 1:
                fill()
        bm = u[CONV_BLOCKS - 2]
        cm = u[CONV_BLOCKS - 1]
        bm_bf = bm.astype(BF16)
        cm_bf = cm.astype(BF16)
        cbm = lax.dot_general(cm_bf, bm_bf, (((1,), (1,)), ((), ())),
                              preferred_element_type=F32)
        dtc = dt_s[s_bslot, rows, :]
        acs = acs_s[s_bslot, rows, :]
        state = state_s[...]
        y_off = jnp.dot(cm_bf, state.astype(BF16), preferred_element_type=F32)

        y_blocks, xdec_blocks, cdec_blocks = [], [], []
        for kb in range(HEADS_PER_GROUP // 2):
            bl = slice(kb * LANES, (kb + 1) * LANES)
            a_cols, d_cols = [], []
            for hh in range(2):
                sel = lane == g * HEADS_PER_GROUP + 2 * kb + hh
                a_cols.append(jnp.sum(jnp.where(sel, acs, 0.0), axis=1, keepdims=True))
                d_cols.append(jnp.sum(jnp.where(sel, dtc, 0.0), axis=1, keepdims=True))
            a_blk = jnp.where(lo, a_cols[0], a_cols[1])
            dt_blk = jnp.where(lo, d_cols[0], d_cols[1])
            a_end = a_blk[CHUNK - 1:CHUNK, :]
            xs_blk = u[kb]
            xdt = xs_blk * dt_blk
            ms = []
            for hh in range(2):
                a_col = a_cols[hh]
                head = g * HEADS_PER_GROUP + 2 * kb + hh
                a_row = acst_s[s_bslot, c, pl.ds(head, 1), :]
                seg = a_col - a_row
                lmat = jnp.exp(jnp.where(causal, seg, -jnp.inf))
                ms.append((cbm * lmat).astype(BF16))
            lhs = jnp.concatenate(ms, axis=1)
            rhs = jnp.concatenate([jnp.where(lo, xdt, 0.0), jnp.where(lo, 0.0, xdt)],
                                  axis=0).astype(BF16)
            y_diag = jnp.dot(lhs, rhs, preferred_element_type=F32)
            y_blocks.append(y_diag + y_off[:, bl] * jnp.exp(a_blk) + xs_blk * dsk[:, bl])
            xdec_blocks.append(xdt * jnp.exp(a_end - a_blk))
            cdec_blocks.append(jnp.exp(a_end))
            if kb % 2 == 1:
                fill()
        y = jnp.concatenate(y_blocks, axis=1)
        xdec = jnp.concatenate(xdec_blocks, axis=1).astype(BF16)
        cdec = jnp.concatenate(cdec_blocks, axis=1)
        st_new = jnp.dot(bm.T.astype(BF16), xdec, preferred_element_type=F32)
        state_s[...] = state * cdec + st_new

        zc = z_s[rows, :]
        v = y * _silu(zc)
        ssq_ref[0, rows, :] += jnp.sum(v * v, axis=1, keepdims=True)
        yg_ref[0, rows, :] = (v * nw).astype(BF16)

    def run(scan_slot, proj_slot):
        def body(c, carry):
            pieces = iter(proj_pieces(c, *slots[proj_slot]) if proj_slot is not None else ())

            calls = [0]

            def fill():
                calls[0] += 1
                if calls[0] % CHUNKS_PER_ITER == 0:
                    piece = next(pieces, None)
                    if piece is not None:
                        piece()

            if scan_slot is not None:
                for cc in range(CHUNKS_PER_ITER):
                    scan_chunk(c * CHUNKS_PER_ITER + cc, *slots[scan_slot], fill)
            for piece in pieces:
                piece()
            return carry
        lax.fori_loop(0, SEQ // PROJ_ROWS, body, 0)

    @pl.when(s == 0)
    def _():
        run(None, 0)

    @pl.when(s > 0)
    def _():
        state_s[...] = jnp.zeros_like(state_s)

        @pl.when(g == 0)
        def _():
            ssq_ref[...] = jnp.zeros_like(ssq_ref)

    for parity in range(2):
        @pl.when((s > 0) & (s < N_ITEMS) & (s % 2 == parity))
        def _():
            run(1 - parity, parity)

    @pl.when(s == N_ITEMS)
    def _():
        run((N_ITEMS - 1) % 2, None)


def _ssd_scan(xb, wz_bf, wxbc_bf, wdt_bf, cw, cb, dtb, alog, dsk, nw):
    proj_b = lambda s: jnp.minimum(s, N_ITEMS - 1) // N_GROUPS
    proj_g = lambda s: jnp.minimum(s, N_ITEMS - 1) % N_GROUPS
    scan_b = lambda s: jnp.maximum(s - 1, 0) // N_GROUPS
    scan_g = lambda s: jnp.maximum(s - 1, 0) % N_GROUPS
    return pl.pallas_call(
        _ssd_kernel,
        grid=(N_ITEMS + 1,),
        in_specs=[
            pl.BlockSpec((1, SEQ, D_MODEL), lambda s: (proj_b(s), 0, 0),
                         pipeline_mode=pl.Buffered(1)),
            pl.BlockSpec((D_MODEL, GROUP_INNER), lambda s: (0, proj_g(s))),
            pl.BlockSpec((D_MODEL, GROUP_CONV), lambda s: (0, proj_g(s))),
            pl.BlockSpec((D_MODEL, LANES), lambda s: (0, 0)),
            pl.BlockSpec((CONV_WIDTH, GROUP_CONV), lambda s: (0, scan_g(s))),
            pl.BlockSpec((1, GROUP_CONV), lambda s: (0, scan_g(s))),
            pl.BlockSpec((1, LANES), lambda s: (0, 0)),
            pl.BlockSpec((1, LANES), lambda s: (0, 0)),
            pl.BlockSpec((1, GROUP_INNER), lambda s: (0, scan_g(s))),
            pl.BlockSpec((1, GROUP_INNER), lambda s: (0, scan_g(s))),
        ],
        out_specs=[
            pl.BlockSpec((1, SEQ, GROUP_INNER), lambda s: (scan_b(s), 0, scan_g(s))),
            pl.BlockSpec((1, SEQ, 1), lambda s: (scan_b(s), 0, 0)),
        ],
        out_shape=[
            jax.ShapeDtypeStruct((BATCH, SEQ, D_INNER), BF16),
            jax.ShapeDtypeStruct((BATCH, SEQ, 1), F32),
        ],
        scratch_shapes=[
            pltpu.VMEM((SEQ, GROUP_INNER), F32),
            pltpu.VMEM((SEQ, GROUP_INNER), F32),
            pltpu.VMEM((CONV_BLOCKS, SEQ + CONV_PAD, LANES), F32),
            pltpu.VMEM((CONV_BLOCKS, SEQ + CONV_PAD, LANES), F32),
            pltpu.VMEM((2, SEQ, LANES), F32),
            pltpu.VMEM((2, SEQ, LANES), F32),
            pltpu.VMEM((2, SEQ // CHUNK, LANES, CHUNK), F32),
            pltpu.VMEM((D_STATE, GROUP_INNER), F32),
        ],
        compiler_params=pltpu.CompilerParams(
            dimension_semantics=("arbitrary",),
            vmem_limit_bytes=VMEM_LIMIT),
        name="ssd_scan",
    )(xb, wz_bf, wxbc_bf, wdt_bf, cw, cb, dtb, alog, dsk, nw)


def _proj_ln_kernel(u_ref, w_ref, ssq_ref, x_ref, g_ref, b_ref, o_ref):
    acc = jnp.dot(u_ref[...], w_ref[...], preferred_element_type=F32)
    r = lax.rsqrt(ssq_ref[...] * (1.0 / D_INNER) + RMS_EPS)
    y = ALPHA * x_ref[...] + acc * r
    o_ref[...] = _layer_norm(y, g_ref[...], b_ref[...])


def _proj_ln(u, w_bf, ssq, x, g, b, *, tm=512):
    n_tok = x.shape[0]
    return pl.pallas_call(
        _proj_ln_kernel,
        grid=(n_tok // tm,),
        in_specs=[
            pl.BlockSpec((tm, D_INNER), lambda i: (i, 0)),
            pl.BlockSpec((D_INNER, D_MODEL), lambda i: (0, 0), pipeline_mode=pl.Buffered(1)),
            pl.BlockSpec((tm, 1), lambda i: (i, 0)),
            pl.BlockSpec((tm, D_MODEL), lambda i: (i, 0)),
            pl.BlockSpec((1, D_MODEL), lambda i: (0, 0)),
            pl.BlockSpec((1, D_MODEL), lambda i: (0, 0)),
        ],
        out_specs=pl.BlockSpec((tm, D_MODEL), lambda i: (i, 0)),
        out_shape=jax.ShapeDtypeStruct((n_tok, D_MODEL), F32),
        compiler_params=pltpu.CompilerParams(
            dimension_semantics=("parallel",),
            vmem_limit_bytes=VMEM_LIMIT),
        name="proj_ln",
    )(u, w_bf, ssq, x, g, b)


def _group_major_conv(a):
    lead = a.shape[:-1]
    xs = a[..., :D_INNER].reshape(lead + (N_GROUPS, GROUP_INNER))
    bs = a[..., D_INNER:D_INNER + N_GROUPS * D_STATE].reshape(lead + (N_GROUPS, D_STATE))
    cs = a[..., D_INNER + N_GROUPS * D_STATE:].reshape(lead + (N_GROUPS, D_STATE))
    return jnp.concatenate([xs, bs, cs], axis=-1).reshape(lead + (N_GROUPS * GROUP_CONV,))


def _pad_heads(a):
    pad = [(0, 0)] * (a.ndim - 1) + [(0, LANES - N_HEADS)]
    return jnp.pad(a, pad)


def kernel(x, p, pool_w, pool_scale, ssm_in_w, ssm_conv_w, ssm_conv_b, ssm_dt_bias,
           ssm_a_log, ssm_d, ssm_norm_w, ssm_out_w, mlp_w1, mlp_w2, ln_g, ln_b,
           ple_w, ple_gate_w):
    n_tok = BATCH * SEQ
    row = lambda a: a.reshape(1, -1)

    x1 = _pool_ln(x, pool_w[0].astype(BF16), row(pool_scale[0]),
                  row(ln_g[0, 0]), row(ln_b[0, 0]))
    x2, x2b = _mlp_ln_ple(
        x1.reshape(n_tok, D_MODEL), mlp_w1[0].astype(BF16), mlp_w2[0].astype(BF16),
        row(ln_g[0, 1]), row(ln_b[0, 1]), p[0].reshape(n_tok, PLE_DIM),
        ple_w[0].astype(BF16), ple_gate_w[0].astype(BF16))

    in_w = ssm_in_w[0]
    wz = in_w[:, :D_INNER].astype(BF16)
    wxbc = _group_major_conv(in_w[:, D_INNER:D_INNER + D_INNER + 2 * N_GROUPS * D_STATE]).astype(BF16)
    wdt = _pad_heads(in_w[:, D_INNER + D_INNER + 2 * N_GROUPS * D_STATE:]).astype(BF16)
    yg, ssq = _ssd_scan(
        x2b.reshape(BATCH, SEQ, D_MODEL), wz, wxbc, wdt,
        _group_major_conv(ssm_conv_w[0]), row(_group_major_conv(ssm_conv_b[0])),
        row(_pad_heads(ssm_dt_bias[0])), row(_pad_heads(ssm_a_log[0])),
        row(jnp.repeat(ssm_d[0], HEAD_DIM)), row(ssm_norm_w[0]))
    x3 = _proj_ln(yg.reshape(n_tok, D_INNER), ssm_out_w[0].astype(BF16),
                  ssq.reshape(n_tok, 1), x2, row(ln_g[1, 0]), row(ln_b[1, 0]))
    x4, _ = _mlp_ln_ple(
        x3, mlp_w1[1].astype(BF16), mlp_w2[1].astype(BF16),
        row(ln_g[1, 1]), row(ln_b[1, 1]), p[1].reshape(n_tok, PLE_DIM),
        ple_w[1].astype(BF16), ple_gate_w[1].astype(BF16))
    return x4.reshape(BATCH, SEQ, D_MODEL)
```

```python
import functools

import jax
import jax.numpy as jnp
from jax import lax
from jax.experimental import pallas as pl
from jax.experimental.pallas import tpu as pltpu

F32 = jnp.float32
BF16 = jnp.bfloat16

D_MODEL = 2048
BATCH = 8
SEQ = 2048
DEPTH = 2
ALPHA = (2.0 * DEPTH) ** 0.25
LN_EPS = 1e-5

POOL_WINDOWS = (2, 4, 8, 16)
POOL_GROUP_DIM = D_MODEL // len(POOL_WINDOWS)
POOL_HALO = 16

D_INNER = 2 * D_MODEL
HEAD_DIM = 64
N_HEADS = D_INNER // HEAD_DIM
N_GROUPS = 8
HEADS_PER_GROUP = N_HEADS // N_GROUPS
D_STATE = 128
CONV_WIDTH = 4
CHUNK = 128
GROUP_INNER = D_INNER // N_GROUPS
GROUP_CONV = GROUP_INNER + 2 * D_STATE
RMS_EPS = 1e-5
N_ITEMS = BATCH * N_GROUPS

D_FF = 4 * D_MODEL
PLE_DIM = 256

LANES = 128
SUBLANES = 8
CONV_PAD = SUBLANES
CONV_BLOCKS = GROUP_CONV // LANES
PROJ_COLS = 256
CHUNKS_PER_ITER = 2
PROJ_ROWS = CHUNKS_PER_ITER * CHUNK
VMEM_LIMIT = 60 * 1024 * 1024


def _silu(x):
    h = 0.5 * x
    return h + h * jnp.tanh(h)


def _layer_norm(y, g, b):
    mu = jnp.mean(y, axis=-1, keepdims=True)
    d = y - mu
    var = jnp.mean(d * d, axis=-1, keepdims=True)
    return d * lax.rsqrt(var + LN_EPS) * g + b


def _pool_ln_kernel(x_ref, halo_ref, w_ref, scale_ref, g_ref, b_ref, o_ref, *, ts):
    i = pl.program_id(1)
    halo_on = (i > 0).astype(F32)
    pos = lax.broadcasted_iota(jnp.int32, (ts, 1), 0) + i * ts
    for gi, win in enumerate(POOL_WINDOWS):
        sl = slice(gi * POOL_GROUP_DIM, (gi + 1) * POOL_GROUP_DIM)
        xg = x_ref[0, :, sl]
        s = jnp.concatenate([halo_ref[0, :, sl] * halo_on, xg], axis=0)
        k = 1
        while k < win:
            s = s + pltpu.roll(s, k, axis=0)
            k *= 2
        cnt = jnp.minimum(pos + 1, win).astype(F32)
        pooled = s[POOL_HALO:] / cnt - xg
        h = jnp.dot(pooled.astype(BF16), w_ref[gi], preferred_element_type=F32)
        o_ref[0, :, sl] = ALPHA * xg + h * scale_ref[:, sl]
    o_ref[0] = _layer_norm(o_ref[0], g_ref[...], b_ref[...])


def _pool_ln(x, w_bf, scale, g, b, *, ts=256):
    nt = SEQ // ts
    hb = ts // POOL_HALO
    return pl.pallas_call(
        functools.partial(_pool_ln_kernel, ts=ts),
        grid=(BATCH, nt),
        in_specs=[
            pl.BlockSpec((1, ts, D_MODEL), lambda b_, i: (b_, i, 0)),
            pl.BlockSpec((1, POOL_HALO, D_MODEL),
                         lambda b_, i: (b_, jnp.maximum(i * hb - 1, 0), 0)),
            pl.BlockSpec((len(POOL_WINDOWS), POOL_GROUP_DIM, POOL_GROUP_DIM),
                         lambda b_, i: (0, 0, 0)),
            pl.BlockSpec((1, D_MODEL), lambda b_, i: (0, 0)),
            pl.BlockSpec((1, D_MODEL), lambda b_, i: (0, 0)),
            pl.BlockSpec((1, D_MODEL), lambda b_, i: (0, 0)),
        ],
        out_specs=pl.BlockSpec((1, ts, D_MODEL), lambda b_, i: (b_, i, 0)),
        out_shape=jax.ShapeDtypeStruct((BATCH, SEQ, D_MODEL), F32),
        compiler_params=pltpu.CompilerParams(
            dimension_semantics=("parallel", "arbitrary"),
            vmem_limit_bytes=VMEM_LIMIT),
        name="pool_ln",
    )(x, x, w_bf, scale, g, b)


def _mlp_kernel(x_ref, w1_ref, w2_ref, g_ref, b_ref, p_ref, plew_ref, gatew_ref,
                o_ref, ob_ref, xb_s, *, n_chunks):
    j = pl.program_id(1)

    @pl.when(j == 0)
    def _():
        x = x_ref[...]
        xb_s[...] = x.astype(BF16)
        o_ref[...] = ALPHA * x

    h = jnp.dot(xb_s[...], w1_ref[...], preferred_element_type=F32)
    h = jnp.maximum(h, 0.0)
    h = h * h
    o_ref[...] += jnp.dot(h.astype(BF16), w2_ref[...], preferred_element_type=F32)

    @pl.when(j == n_chunks - 1)
    def _():
        x2 = _layer_norm(o_ref[...], g_ref[...], b_ref[...])
        gate = jax.nn.sigmoid(
            jnp.dot(x2.astype(BF16), gatew_ref[...], preferred_element_type=F32))
        pe = jnp.dot(p_ref[...].astype(BF16), plew_ref[...], preferred_element_type=F32)
        out = x2 + gate * pe
        o_ref[...] = out
        ob_ref[...] = out.astype(BF16)


def _mlp_ln_ple(x, w1_bf, w2_bf, g, b, p, plew_bf, gatew_bf, *, tm=512, tf=1024):
    n_tok = x.shape[0]
    n_chunks = D_FF // tf
    return pl.pallas_call(
        functools.partial(_mlp_kernel, n_chunks=n_chunks),
        grid=(n_tok // tm, n_chunks),
        in_specs=[
            pl.BlockSpec((tm, D_MODEL), lambda i, j: (i, 0)),
            pl.BlockSpec((D_MODEL, tf), lambda i, j: (0, j)),
            pl.BlockSpec((tf, D_MODEL), lambda i, j: (j, 0)),
            pl.BlockSpec((1, D_MODEL), lambda i, j: (0, 0)),
            pl.BlockSpec((1, D_MODEL), lambda i, j: (0, 0)),
            pl.BlockSpec((tm, PLE_DIM), lambda i, j: (i, 0)),
            pl.BlockSpec((PLE_DIM, D_MODEL), lambda i, j: (0, 0),
                         pipeline_mode=pl.Buffered(1)),
            pl.BlockSpec((D_MODEL, D_MODEL), lambda i, j: (0, 0),
                         pipeline_mode=pl.Buffered(1)),
        ],
        out_specs=[
            pl.BlockSpec((tm, D_MODEL), lambda i, j: (i, 0)),
            pl.BlockSpec((tm, D_MODEL), lambda i, j: (i, 0)),
        ],
        out_shape=[
            jax.ShapeDtypeStruct((n_tok, D_MODEL), F32),
            jax.ShapeDtypeStruct((n_tok, D_MODEL), BF16),
        ],
        scratch_shapes=[pltpu.VMEM((tm, D_MODEL), BF16)],
        compiler_params=pltpu.CompilerParams(
            dimension_semantics=("parallel", "arbitrary"),
            vmem_limit_bytes=VMEM_LIMIT),
        name="mlp_ln_ple",
    )(x, w1_bf, w2_bf, g, b, p, plew_bf, gatew_bf)


def _ssd_kernel(xb_ref, wz_ref, wx_ref, wb_ref, wc_ref, wdt_ref, cwx_ref, cwb_ref, cwc_ref,
                cbx_ref, cbb_ref, cbc_ref, dtb_ref, alog_ref, dsk_ref, nw_ref,
                yg_ref, ssq_ref, z0_s, z1_s, xbc0_s, xbc1_s, wbc_s, dt_s, acs_s, acst_s,
                state_s):
    s = pl.program_id(0)
    t = jnp.minimum(s, N_ITEMS - 1)
    t_bslot = (t // N_GROUPS) & 1
    sc = jnp.maximum(s - 1, 0)
    g = sc % N_GROUPS
    s_bslot = (sc // N_GROUPS) & 1
    slots = ((z0_s, xbc0_s), (z1_s, xbc1_s))

    @pl.when((t % N_GROUPS == 0) & (s < N_ITEMS))
    def _():
        dt_raw = jnp.dot(xb_ref[0], wdt_ref[...], preferred_element_type=F32) + dtb_ref[...]
        dt = jnp.maximum(dt_raw, 0.0) + jnp.log1p(jnp.exp(-jnp.abs(dt_raw)))
        dt_s[t_bslot] = dt
        a = dt * (-jnp.exp(alog_ref[...]))
        in_chunk = lax.broadcasted_iota(jnp.int32, (SEQ, 1), 0) & (CHUNK - 1)
        k = 1
        while k < CHUNK:
            a = a + jnp.where(in_chunk >= k, pltpu.roll(a, k, axis=0), 0.0)
            k *= 2
        acs_s[t_bslot] = a
        for c in range(SEQ // CHUNK):
            acst_s[t_bslot, c] = a[c * CHUNK:(c + 1) * CHUNK, :].T

    @pl.when(s < N_ITEMS)
    def _():
        wbc_s[:, :D_STATE] = wb_ref[...]
        wbc_s[:, D_STATE:] = wc_ref[...]

    @pl.when(s == 0)
    def _():
        for _, xbc_s in slots:
            for j in range(CONV_BLOCKS):
                xbc_s[j, 0:CONV_PAD, :] = jnp.zeros((CONV_PAD, LANES), F32)

    def proj_pieces(c, z_s, xbc_s):
        r0 = pl.multiple_of(c * PROJ_ROWS, PROJ_ROWS)
        xr = xb_ref[0, pl.ds(r0, PROJ_ROWS), :]

        def z_piece(n):
            cols = slice(n * PROJ_COLS, (n + 1) * PROJ_COLS)
            z_s[pl.ds(r0, PROJ_ROWS), cols] = jnp.dot(
                xr, wz_ref[:, cols], preferred_element_type=F32)

        def xbc_piece(n):
            if n < GROUP_INNER // PROJ_COLS:
                w = wx_ref[:, n * PROJ_COLS:(n + 1) * PROJ_COLS]
            else:
                w = wbc_s[...]
            res = jnp.dot(xr, w, preferred_element_type=F32)
            for jj in range(PROJ_COLS // LANES):
                j = n * (PROJ_COLS // LANES) + jj
                xbc_s[j, pl.ds(r0 + CONV_PAD, PROJ_ROWS), :] = res[:, jj * LANES:(jj + 1) * LANES]

        return ([functools.partial(z_piece, n) for n in range(GROUP_INNER // PROJ_COLS)]
                + [functools.partial(xbc_piece, n) for n in range(GROUP_CONV // PROJ_COLS)])

    lane = lax.broadcasted_iota(jnp.int32, (1, LANES), 1)
    lo = lane < HEAD_DIM
    ri = lax.broadcasted_iota(jnp.int32, (CHUNK, CHUNK), 0)
    ci = lax.broadcasted_iota(jnp.int32, (CHUNK, CHUNK), 1)
    causal = ri >= ci


    def conv_chunk(c, xbc_s, fill):
        cw = jnp.concatenate([cwx_ref[...], cwb_ref[...], cwc_ref[...]], axis=1)
        cb = jnp.concatenate([cbx_ref[...], cbb_ref[...], cbc_ref[...]], axis=1)
        r0 = pl.multiple_of(c * CHUNK, CHUNK)
        u = []
        for j in range(CONV_BLOCKS):
            bl = slice(j * LANES, (j + 1) * LANES)
            acc = cb[:, bl]
            for k in range(CONV_WIDTH):
                off = CONV_PAD - (CONV_WIDTH - 1) + k
                acc = acc + xbc_s[j, pl.ds(r0 + off, CHUNK), :] * cw[k:k + 1, bl]
            u.append(_silu(acc))
            if j % 2 == 1:
                fill()
        return u

    def scan_chunk(c, u, z_s, fill):
        dsk = dsk_ref[...]
        nw = nw_ref[...]
        r0 = pl.multiple_of(c * CHUNK, CHUNK)
        rows = pl.ds(r0, CHUNK)
        bm = u[CONV_BLOCKS - 2]
        cm = u[CONV_BLOCKS - 1]
        bm_bf = bm.astype(BF16)
        cm_bf = cm.astype(BF16)
        cbm = lax.dot_general(cm_bf, bm_bf, (((1,), (1,)), ((), ())),
                              preferred_element_type=F32)
        dtc = dt_s[s_bslot, rows, :]
        acs = acs_s[s_bslot, rows, :]
        state = state_s[...]
        y_off = jnp.dot(cm_bf, state.astype(BF16), preferred_element_type=F32)

        y_blocks, xdec_blocks, cdec_blocks = [], [], []
        for kb in range(HEADS_PER_GROUP // 2):
            bl = slice(kb * LANES, (kb + 1) * LANES)
            a_cols, d_cols = [], []
            for hh in range(2):
                sel = lane == g * HEADS_PER_GROUP + 2 * kb + hh
                a_cols.append(jnp.sum(jnp.where(sel, acs, 0.0), axis=1, keepdims=True))
                d_cols.append(jnp.sum(jnp.where(sel, dtc, 0.0), axis=1, keepdims=True))
            a_blk = jnp.where(lo, a_cols[0], a_cols[1])
            dt_blk = jnp.where(lo, d_cols[0], d_cols[1])
            a_end = a_blk[CHUNK - 1:CHUNK, :]
            xs_blk = u[kb]
            xdt = xs_blk * dt_blk
            ms = []
            for hh in range(2):
                a_col = a_cols[hh]
                head = g * HEADS_PER_GROUP + 2 * kb + hh
                a_row = acst_s[s_bslot, c, pl.ds(head, 1), :]
                seg = a_col - a_row
                lmat = jnp.exp(jnp.where(causal, seg, -jnp.inf))
                ms.append((cbm * lmat).astype(BF16))
            lhs = jnp.concatenate(ms, axis=1)
            rhs = jnp.concatenate([jnp.where(lo, xdt, 0.0), jnp.where(lo, 0.0, xdt)],
                                  axis=0).astype(BF16)
            y_diag = jnp.dot(lhs, rhs, preferred_element_type=F32)
            y_blocks.append(y_diag + y_off[:, bl] * jnp.exp(a_blk) + xs_blk * dsk[:, bl])
            xdec_blocks.append(xdt * jnp.exp(a_end - a_blk))
            cdec_blocks.append(jnp.exp(a_end))
            if kb % 2 == 1:
                fill()
        y = jnp.concatenate(y_blocks, axis=1)
        xdec = jnp.concatenate(xdec_blocks, axis=1).astype(BF16)
        cdec = jnp.concatenate(cdec_blocks, axis=1)
        st_new = jnp.dot(bm.T.astype(BF16), xdec, preferred_element_type=F32)
        state_s[...] = state * cdec + st_new

        zc = z_s[rows, :]
        v = y * _silu(zc)
        ssq_ref[0, rows, :] += jnp.sum(v * v, axis=1, keepdims=True)
        yg_ref[0, rows, :] = (v * nw).astype(BF16)

    def run(scan_slot, proj_slot):
        def body(c, carry):
            pieces = iter(proj_pieces(c, *slots[proj_slot]) if proj_slot is not None else ())

            calls = [0]

            def fill():
                calls[0] += 1
                if calls[0] % CHUNKS_PER_ITER == 0:
                    piece = next(pieces, None)
                    if piece is not None:
                        piece()

            if scan_slot is not None:
                z_s, xbc_s = slots[scan_slot]
                chunks = [c * CHUNKS_PER_ITER + cc for cc in range(CHUNKS_PER_ITER)]
                us = [conv_chunk(ch, xbc_s, fill) for ch in chunks]
                for ch, u in zip(chunks, us):
                    scan_chunk(ch, u, z_s, fill)
            for piece in pieces:
                piece()
            return carry
        lax.fori_loop(0, SEQ // PROJ_ROWS, body, 0)

    @pl.when(s == 0)
    def _():
        run(None, 0)

    @pl.when(s > 0)
    def _():
        state_s[...] = jnp.zeros_like(state_s)

        @pl.when(g == 0)
        def _():
            ssq_ref[...] = jnp.zeros_like(ssq_ref)

    for parity in range(2):
        @pl.when((s > 0) & (s < N_ITEMS) & (s % 2 == parity))
        def _():
            run(1 - parity, parity)

    @pl.when(s == N_ITEMS)
    def _():
        run((N_ITEMS - 1) % 2, None)


def _ssd_scan(xb, in_w_bf, wdt_bf, cw, cb, dtb, alog, dsk, nw):
    x_blk0 = D_INNER // GROUP_INNER
    b_blk0 = 2 * D_INNER // D_STATE
    c_blk0 = b_blk0 + N_GROUPS
    cb_blk0 = D_INNER // D_STATE
    cc_blk0 = cb_blk0 + N_GROUPS
    proj_b = lambda s: jnp.minimum(s, N_ITEMS - 1) // N_GROUPS
    proj_g = lambda s: jnp.minimum(s, N_ITEMS - 1) % N_GROUPS
    scan_b = lambda s: jnp.maximum(s - 1, 0) // N_GROUPS
    scan_g = lambda s: jnp.maximum(s - 1, 0) % N_GROUPS
    return pl.pallas_call(
        _ssd_kernel,
        grid=(N_ITEMS + 1,),
        in_specs=[
            pl.BlockSpec((1, SEQ, D_MODEL), lambda s: (proj_b(s), 0, 0),
                         pipeline_mode=pl.Buffered(1)),
            pl.BlockSpec((D_MODEL, GROUP_INNER), lambda s: (0, proj_g(s))),
            pl.BlockSpec((D_MODEL, GROUP_INNER), lambda s: (0, x_blk0 + proj_g(s))),
            pl.BlockSpec((D_MODEL, D_STATE), lambda s: (0, b_blk0 + proj_g(s))),
            pl.BlockSpec((D_MODEL, D_STATE), lambda s: (0, c_blk0 + proj_g(s))),
            pl.BlockSpec((D_MODEL, LANES), lambda s: (0, 0)),
            pl.BlockSpec((CONV_WIDTH, GROUP_INNER), lambda s: (0, scan_g(s))),
            pl.BlockSpec((CONV_WIDTH, D_STATE), lambda s: (0, cb_blk0 + scan_g(s))),
            pl.BlockSpec((CONV_WIDTH, D_STATE), lambda s: (0, cc_blk0 + scan_g(s))),
            pl.BlockSpec((1, GROUP_INNER), lambda s: (0, scan_g(s))),
            pl.BlockSpec((1, D_STATE), lambda s: (0, cb_blk0 + scan_g(s))),
            pl.BlockSpec((1, D_STATE), lambda s: (0, cc_blk0 + scan_g(s))),
            pl.BlockSpec((1, LANES), lambda s: (0, 0)),
            pl.BlockSpec((1, LANES), lambda s: (0, 0)),
            pl.BlockSpec((1, GROUP_INNER), lambda s: (0, scan_g(s))),
            pl.BlockSpec((1, GROUP_INNER), lambda s: (0, scan_g(s))),
        ],
        out_specs=[
            pl.BlockSpec((1, SEQ, GROUP_INNER), lambda s: (scan_b(s), 0, scan_g(s))),
            pl.BlockSpec((1, SEQ, 1), lambda s: (scan_b(s), 0, 0)),
        ],
        out_shape=[
            jax.ShapeDtypeStruct((BATCH, SEQ, D_INNER), BF16),
            jax.ShapeDtypeStruct((BATCH, SEQ, 1), F32),
        ],
        scratch_shapes=[
            pltpu.VMEM((SEQ, GROUP_INNER), F32),
            pltpu.VMEM((SEQ, GROUP_INNER), F32),
            pltpu.VMEM((CONV_BLOCKS, SEQ + CONV_PAD, LANES), F32),
            pltpu.VMEM((CONV_BLOCKS, SEQ + CONV_PAD, LANES), F32),
            pltpu.VMEM((D_MODEL, 2 * D_STATE), BF16),
            pltpu.VMEM((2, SEQ, LANES), F32),
            pltpu.VMEM((2, SEQ, LANES), F32),
            pltpu.VMEM((2, SEQ // CHUNK, LANES, CHUNK), F32),
            pltpu.VMEM((D_STATE, GROUP_INNER), F32),
        ],
        compiler_params=pltpu.CompilerParams(
            dimension_semantics=("arbitrary",),
            vmem_limit_bytes=VMEM_LIMIT),
        name="ssd_scan",
    )(xb, in_w_bf, in_w_bf, in_w_bf, in_w_bf, wdt_bf, cw, cw, cw, cb, cb, cb, dtb, alog, dsk, nw)


def _proj_ln_kernel(u_ref, w_ref, ssq_ref, x_ref, g_ref, b_ref, o_ref):
    acc = jnp.dot(u_ref[...], w_ref[...], preferred_element_type=F32)
    r = lax.rsqrt(ssq_ref[...] * (1.0 / D_INNER) + RMS_EPS)
    y = ALPHA * x_ref[...] + acc * r
    o_ref[...] = _layer_norm(y, g_ref[...], b_ref[...])


def _proj_ln(u, w_bf, ssq, x, g, b, *, tm=512):
    n_tok = x.shape[0]
    return pl.pallas_call(
        _proj_ln_kernel,
        grid=(n_tok // tm,),
        in_specs=[
            pl.BlockSpec((tm, D_INNER), lambda i: (i, 0)),
            pl.BlockSpec((D_INNER, D_MODEL), lambda i: (0, 0), pipeline_mode=pl.Buffered(1)),
            pl.BlockSpec((tm, 1), lambda i: (i, 0)),
            pl.BlockSpec((tm, D_MODEL), lambda i: (i, 0)),
            pl.BlockSpec((1, D_MODEL), lambda i: (0, 0)),
            pl.BlockSpec((1, D_MODEL), lambda i: (0, 0)),
        ],
        out_specs=pl.BlockSpec((tm, D_MODEL), lambda i: (i, 0)),
        out_shape=jax.ShapeDtypeStruct((n_tok, D_MODEL), F32),
        compiler_params=pltpu.CompilerParams(
            dimension_semantics=("parallel",),
            vmem_limit_bytes=VMEM_LIMIT),
        name="proj_ln",
    )(u, w_bf, ssq, x, g, b)


def _pad_heads(a):
    pad = [(0, 0)] * (a.ndim - 1) + [(0, LANES - N_HEADS)]
    return jnp.pad(a, pad)


def kernel(x, p, pool_w, pool_scale, ssm_in_w, ssm_conv_w, ssm_conv_b, ssm_dt_bias,
           ssm_a_log, ssm_d, ssm_norm_w, ssm_out_w, mlp_w1, mlp_w2, ln_g, ln_b,
           ple_w, ple_gate_w):
    n_tok = BATCH * SEQ
    row = lambda a: a.reshape(1, -1)

    x1 = _pool_ln(x, pool_w[0].astype(BF16), row(pool_scale[0]),
                  row(ln_g[0, 0]), row(ln_b[0, 0]))
    x2, x2b = _mlp_ln_ple(
        x1.reshape(n_tok, D_MODEL), mlp_w1[0].astype(BF16), mlp_w2[0].astype(BF16),
        row(ln_g[0, 1]), row(ln_b[0, 1]), p[0].reshape(n_tok, PLE_DIM),
        ple_w[0].astype(BF16), ple_gate_w[0].astype(BF16))

    in_w_bf = ssm_in_w[0].astype(BF16)
    wdt = _pad_heads(in_w_bf[:, 2 * D_INNER + 2 * N_GROUPS * D_STATE:])
    yg, ssq = _ssd_scan(
        x2b.reshape(BATCH, SEQ, D_MODEL), in_w_bf, wdt, ssm_conv_w[0], row(ssm_conv_b[0]),
        row(_pad_heads(ssm_dt_bias[0])), row(_pad_heads(ssm_a_log[0])),
        row(jnp.repeat(ssm_d[0], HEAD_DIM)), row(ssm_norm_w[0]))
    x3 = _proj_ln(yg.reshape(n_tok, D_INNER), ssm_out_w[0].astype(BF16),
                  ssq.reshape(n_tok, 1), x2, row(ln_g[1, 0]), row(ln_b[1, 0]))
    x4, _ = _mlp_ln_ple(
        x3, mlp_w1[1].astype(BF16), mlp_w2[1].astype(BF16),
        row(ln_g[1, 1]), row(ln_b[1, 1]), p[1].reshape(n_tok, PLE_DIM),
        ple_w[1].astype(BF16), ple_gate_w[1].astype(BF16))
    return x4.reshape(BATCH, SEQ, D_MODEL)
```

```python
import functools

import jax
import jax.numpy as jnp
from jax import lax
from jax.experimental import pallas as pl
from jax.experimental.pallas import tpu as pltpu

F32 = jnp.float32
BF16 = jnp.bfloat16

D_MODEL = 2048
BATCH = 8
SEQ = 2048
DEPTH = 2
ALPHA = (2.0 * DEPTH) ** 0.25
LN_EPS = 1e-5

POOL_WINDOWS = (2, 4, 8, 16)
POOL_GROUP_DIM = D_MODEL // len(POOL_WINDOWS)
POOL_HALO = 16

D_INNER = 2 * D_MODEL
HEAD_DIM = 64
N_HEADS = D_INNER // HEAD_DIM
N_GROUPS = 8
HEADS_PER_GROUP = N_HEADS // N_GROUPS
D_STATE = 128
CONV_WIDTH = 4
CHUNK = 128
GROUP_INNER = D_INNER // N_GROUPS
GROUP_CONV = GROUP_INNER + 2 * D_STATE
RMS_EPS = 1e-5
N_ITEMS = BATCH * N_GROUPS

D_FF = 4 * D_MODEL
PLE_DIM = 256

LANES = 128
SUBLANES = 8
CONV_PAD = SUBLANES
CONV_BLOCKS = GROUP_CONV // LANES
PROJ_COLS = 256
CHUNKS_PER_ITER = 2
PROJ_ROWS = CHUNKS_PER_ITER * CHUNK
VMEM_LIMIT = 60 * 1024 * 1024


def _silu(x):
    h = 0.5 * x
    return h + h * jnp.tanh(h)


def _layer_norm(y, g, b):
    mu = jnp.mean(y, axis=-1, keepdims=True)
    d = y - mu
    var = jnp.mean(d * d, axis=-1, keepdims=True)
    return d * lax.rsqrt(var + LN_EPS) * g + b


def _pool_ln_kernel(x_ref, halo_ref, w_ref, scale_ref, g_ref, b_ref, o_ref, *, ts):
    i = pl.program_id(1)
    halo_on = (i > 0).astype(F32)
    pos = lax.broadcasted_iota(jnp.int32, (ts, 1), 0) + i * ts
    for gi, win in enumerate(POOL_WINDOWS):
        sl = slice(gi * POOL_GROUP_DIM, (gi + 1) * POOL_GROUP_DIM)
        xg = x_ref[0, :, sl]
        s = jnp.concatenate([halo_ref[0, :, sl] * halo_on, xg], axis=0)
        k = 1
        while k < win:
            s = s + pltpu.roll(s, k, axis=0)
            k *= 2
        cnt = jnp.minimum(pos + 1, win).astype(F32)
        pooled = s[POOL_HALO:] / cnt - xg
        h = jnp.dot(pooled.astype(BF16), w_ref[gi], preferred_element_type=F32)
        o_ref[0, :, sl] = ALPHA * xg + h * scale_ref[:, sl]
    o_ref[0] = _layer_norm(o_ref[0], g_ref[...], b_ref[...])


def _pool_ln(x, w_bf, scale, g, b, *, ts=256):
    nt = SEQ // ts
    hb = ts // POOL_HALO
    return pl.pallas_call(
        functools.partial(_pool_ln_kernel, ts=ts),
        grid=(BATCH, nt),
        in_specs=[
            pl.BlockSpec((1, ts, D_MODEL), lambda b_, i: (b_, i, 0)),
            pl.BlockSpec((1, POOL_HALO, D_MODEL),
                         lambda b_, i: (b_, jnp.maximum(i * hb - 1, 0), 0)),
            pl.BlockSpec((len(POOL_WINDOWS), POOL_GROUP_DIM, POOL_GROUP_DIM),
                         lambda b_, i: (0, 0, 0)),
            pl.BlockSpec((1, D_MODEL), lambda b_, i: (0, 0)),
            pl.BlockSpec((1, D_MODEL), lambda b_, i: (0, 0)),
            pl.BlockSpec((1, D_MODEL), lambda b_, i: (0, 0)),
        ],
        out_specs=pl.BlockSpec((1, ts, D_MODEL), lambda b_, i: (b_, i, 0)),
        out_shape=jax.ShapeDtypeStruct((BATCH, SEQ, D_MODEL), F32),
        compiler_params=pltpu.CompilerParams(
            dimension_semantics=("parallel", "arbitrary"),
            vmem_limit_bytes=VMEM_LIMIT),
        name="pool_ln",
    )(x, x, w_bf, scale, g, b)


def _mlp_kernel(x_ref, w1_ref, w2_ref, g_ref, b_ref, p_ref, plew_ref, gatew_ref,
                o_ref, ob_ref, xb_s, *, n_chunks):
    j = pl.program_id(1)

    @pl.when(j == 0)
    def _():
        x = x_ref[...]
        xb_s[...] = x.astype(BF16)
        o_ref[...] = ALPHA * x

    h = jnp.dot(xb_s[...], w1_ref[...], preferred_element_type=F32)
    h = jnp.maximum(h, 0.0)
    h = h * h
    o_ref[...] += jnp.dot(h.astype(BF16), w2_ref[...], preferred_element_type=F32)

    @pl.when(j == n_chunks - 1)
    def _():
        x2 = _layer_norm(o_ref[...], g_ref[...], b_ref[...])
        gate = jax.nn.sigmoid(
            jnp.dot(x2.astype(BF16), gatew_ref[...], preferred_element_type=F32))
        pe = jnp.dot(p_ref[...].astype(BF16), plew_ref[...], preferred_element_type=F32)
        out = x2 + gate * pe
        o_ref[...] = out
        ob_ref[...] = out.astype(BF16)


def _mlp_ln_ple(layer, x, w1_bf, w2_bf, g, b, p, plew_bf, gatew_bf, *, tm=512, tf=1024):
    n_tok = x.shape[0]
    n_chunks = D_FF // tf
    return pl.pallas_call(
        functools.partial(_mlp_kernel, n_chunks=n_chunks),
        grid=(n_tok // tm, n_chunks),
        in_specs=[
            pl.BlockSpec((tm, D_MODEL), lambda i, j: (i, 0)),
            pl.BlockSpec((None, D_MODEL, tf), lambda i, j: (layer, 0, j)),
            pl.BlockSpec((None, tf, D_MODEL), lambda i, j: (layer, j, 0)),
            pl.BlockSpec((1, D_MODEL), lambda i, j: (0, 0)),
            pl.BlockSpec((1, D_MODEL), lambda i, j: (0, 0)),
            pl.BlockSpec((None, tm, PLE_DIM), lambda i, j: (layer, i, 0)),
            pl.BlockSpec((None, PLE_DIM, D_MODEL), lambda i, j: (layer, 0, 0),
                         pipeline_mode=pl.Buffered(1)),
            pl.BlockSpec((None, D_MODEL, D_MODEL), lambda i, j: (layer, 0, 0),
                         pipeline_mode=pl.Buffered(1)),
        ],
        out_specs=[
            pl.BlockSpec((tm, D_MODEL), lambda i, j: (i, 0)),
            pl.BlockSpec((tm, D_MODEL), lambda i, j: (i, 0)),
        ],
        out_shape=[
            jax.ShapeDtypeStruct((n_tok, D_MODEL), F32),
            jax.ShapeDtypeStruct((n_tok, D_MODEL), BF16),
        ],
        scratch_shapes=[pltpu.VMEM((tm, D_MODEL), BF16)],
        compiler_params=pltpu.CompilerParams(
            dimension_semantics=("parallel", "arbitrary"),
            vmem_limit_bytes=VMEM_LIMIT),
        name="mlp_ln_ple",
    )(x, w1_bf, w2_bf, g, b, p, plew_bf, gatew_bf)


def _ssd_kernel(xb_ref, wz_ref, wx_ref, wb_ref, wc_ref, wdt_ref, cwx_ref, cwb_ref, cwc_ref,
                cbx_ref, cbb_ref, cbc_ref, dtb_ref, alog_ref, dsk_ref, nw_ref,
                yg_ref, ssq_ref, z0_s, z1_s, xbc0_s, xbc1_s, wbc_s, dtt_s, acst_s, state_s):
    s = pl.program_id(0)
    t = jnp.minimum(s, N_ITEMS - 1)
    t_bslot = (t // N_GROUPS) & 1
    sc = jnp.maximum(s - 1, 0)
    g = sc % N_GROUPS
    s_bslot = (sc // N_GROUPS) & 1
    slots = ((z0_s, xbc0_s), (z1_s, xbc1_s))

    @pl.when((t % N_GROUPS == 0) & (s < N_ITEMS))
    def _():
        dt_raw = jnp.dot(xb_ref[0], wdt_ref[...], preferred_element_type=F32) + dtb_ref[...]
        dt = jnp.maximum(dt_raw, 0.0) + jnp.log1p(jnp.exp(-jnp.abs(dt_raw)))
        a = dt * (-jnp.exp(alog_ref[...]))
        in_chunk = lax.broadcasted_iota(jnp.int32, (SEQ, 1), 0) & (CHUNK - 1)
        k = 1
        while k < CHUNK:
            a = a + jnp.where(in_chunk >= k, pltpu.roll(a, k, axis=0), 0.0)
            k *= 2
        for c in range(SEQ // CHUNK):
            dtt_s[t_bslot, c] = dt[c * CHUNK:(c + 1) * CHUNK, :].T
            acst_s[t_bslot, c] = a[c * CHUNK:(c + 1) * CHUNK, :].T

    @pl.when(s < N_ITEMS)
    def _():
        wbc_s[:, :D_STATE] = wb_ref[...]
        wbc_s[:, D_STATE:] = wc_ref[...]

    @pl.when(s == 0)
    def _():
        for _, xbc_s in slots:
            for j in range(CONV_BLOCKS):
                xbc_s[j, 0:CONV_PAD, :] = jnp.zeros((CONV_PAD, LANES), F32)

    def proj_pieces(c, z_s, xbc_s):
        r0 = pl.multiple_of(c * PROJ_ROWS, PROJ_ROWS)
        xr = xb_ref[0, pl.ds(r0, PROJ_ROWS), :]

        def z_piece(n):
            cols = slice(n * PROJ_COLS, (n + 1) * PROJ_COLS)
            z_s[pl.ds(r0, PROJ_ROWS), cols] = jnp.dot(
                xr, wz_ref[:, cols], preferred_element_type=F32)

        def xbc_piece(n):
            if n < GROUP_INNER // PROJ_COLS:
                w = wx_ref[:, n * PROJ_COLS:(n + 1) * PROJ_COLS]
            else:
                w = wbc_s[...]
            res = jnp.dot(xr, w, preferred_element_type=F32)
            for jj in range(PROJ_COLS // LANES):
                j = n * (PROJ_COLS // LANES) + jj
                xbc_s[j, pl.ds(r0 + CONV_PAD, PROJ_ROWS), :] = res[:, jj * LANES:(jj + 1) * LANES]

        return ([functools.partial(z_piece, n) for n in range(GROUP_INNER // PROJ_COLS)]
                + [functools.partial(xbc_piece, n) for n in range(GROUP_CONV // PROJ_COLS)])

    lo = lax.broadcasted_iota(jnp.int32, (1, LANES), 1) < HEAD_DIM
    ri = lax.broadcasted_iota(jnp.int32, (CHUNK, CHUNK), 0)
    ci = lax.broadcasted_iota(jnp.int32, (CHUNK, CHUNK), 1)
    causal = ri >= ci


    def conv_chunk(c, xbc_s, fill):
        cw = jnp.concatenate([cwx_ref[...], cwb_ref[...], cwc_ref[...]], axis=1)
        cb = jnp.concatenate([cbx_ref[...], cbb_ref[...], cbc_ref[...]], axis=1)
        r0 = pl.multiple_of(c * CHUNK, CHUNK)
        u = []
        for j in range(CONV_BLOCKS):
            bl = slice(j * LANES, (j + 1) * LANES)
            acc = cb[:, bl]
            for k in range(CONV_WIDTH):
                off = CONV_PAD - (CONV_WIDTH - 1) + k
                acc = acc + xbc_s[j, pl.ds(r0 + off, CHUNK), :] * cw[k:k + 1, bl]
            u.append(_silu(acc))
            if j % 2 == 1:
                fill()
        return u

    def head_pair_columns(rows):
        stacked = jnp.concatenate(
            [jnp.broadcast_to(r, (HEAD_DIM, CHUNK)) for r in rows], axis=0)
        return stacked.T

    def scan_chunk(c, u, z_s, fill):
        dsk = dsk_ref[...]
        nw = nw_ref[...]
        r0 = pl.multiple_of(c * CHUNK, CHUNK)
        rows = pl.ds(r0, CHUNK)
        bm = u[CONV_BLOCKS - 2]
        cm = u[CONV_BLOCKS - 1]
        bm_bf = bm.astype(BF16)
        cm_bf = cm.astype(BF16)
        cbm = lax.dot_general(cm_bf, bm_bf, (((1,), (1,)), ((), ())),
                              preferred_element_type=F32)
        state = state_s[...]
        y_off = jnp.dot(cm_bf, state.astype(BF16), preferred_element_type=F32)

        y_blocks, xdec_blocks, cdec_blocks = [], [], []
        for kb in range(HEADS_PER_GROUP // 2):
            bl = slice(kb * LANES, (kb + 1) * LANES)
            heads = [g * HEADS_PER_GROUP + 2 * kb + hh for hh in range(2)]
            a_rows = [acst_s[s_bslot, c, pl.ds(h, 1), :] for h in heads]
            d_rows = [dtt_s[s_bslot, c, pl.ds(h, 1), :] for h in heads]
            a_blk = head_pair_columns(a_rows)
            dt_blk = head_pair_columns(d_rows)
            a_end = a_blk[CHUNK - 1:CHUNK, :]
            xs_blk = u[kb]
            xdt = xs_blk * dt_blk
            ms = []
            for hh in range(2):
                a_row = a_rows[hh]
                a_col = jnp.broadcast_to(a_row, (CHUNK, CHUNK)).T
                seg = a_col - a_row
                lmat = jnp.exp(jnp.where(causal, seg, -jnp.inf))
                ms.append((cbm * lmat).astype(BF16))
            lhs = jnp.concatenate(ms, axis=1)
            rhs = jnp.concatenate([jnp.where(lo, xdt, 0.0), jnp.where(lo, 0.0, xdt)],
                                  axis=0).astype(BF16)
            y_diag = jnp.dot(lhs, rhs, preferred_element_type=F32)
            y_blocks.append(y_diag + y_off[:, bl] * jnp.exp(a_blk) + xs_blk * dsk[:, bl])
            xdec_blocks.append(xdt * jnp.exp(a_end - a_blk))
            cdec_blocks.append(jnp.exp(a_end))
            if kb % 2 == 1:
                fill()
        y = jnp.concatenate(y_blocks, axis=1)
        xdec = jnp.concatenate(xdec_blocks, axis=1).astype(BF16)
        cdec = jnp.concatenate(cdec_blocks, axis=1)
        st_new = jnp.dot(bm.T.astype(BF16), xdec, preferred_element_type=F32)
        state_s[...] = state * cdec + st_new

        zc = z_s[rows, :]
        v = y * _silu(zc)
        ssq_ref[0, rows, :] += jnp.sum(v * v, axis=1, keepdims=True)
        yg_ref[0, rows, :] = (v * nw).astype(BF16)

    def run(scan_slot, proj_slot):
        def body(c, carry):
            pieces = iter(proj_pieces(c, *slots[proj_slot]) if proj_slot is not None else ())

            calls = [0]

            def fill():
                calls[0] += 1
                if calls[0] % CHUNKS_PER_ITER == 0:
                    piece = next(pieces, None)
                    if piece is not None:
                        piece()

            if scan_slot is not None:
                z_s, xbc_s = slots[scan_slot]
                chunks = [c * CHUNKS_PER_ITER + cc for cc in range(CHUNKS_PER_ITER)]
                us = [conv_chunk(ch, xbc_s, fill) for ch in chunks]
                for ch, u in zip(chunks, us):
                    scan_chunk(ch, u, z_s, fill)
            for piece in pieces:
                piece()
            return carry
        lax.fori_loop(0, SEQ // PROJ_ROWS, body, 0)

    @pl.when(s == 0)
    def _():
        run(None, 0)

    @pl.when(s > 0)
    def _():
        state_s[...] = jnp.zeros_like(state_s)

        @pl.when(g == 0)
        def _():
            ssq_ref[...] = jnp.zeros_like(ssq_ref)

    for parity in range(2):
        @pl.when((s > 0) & (s < N_ITEMS) & (s % 2 == parity))
        def _():
            run(1 - parity, parity)

    @pl.when(s == N_ITEMS)
    def _():
        run((N_ITEMS - 1) % 2, None)


def _ssd_scan(xb, in_w_bf, wdt_bf, cw, cb, dtb, alog, dsk, nw):
    x_blk0 = D_INNER // GROUP_INNER
    b_blk0 = 2 * D_INNER // D_STATE
    c_blk0 = b_blk0 + N_GROUPS
    cb_blk0 = D_INNER // D_STATE
    cc_blk0 = cb_blk0 + N_GROUPS
    proj_b = lambda s: jnp.minimum(s, N_ITEMS - 1) // N_GROUPS
    proj_g = lambda s: jnp.minimum(s, N_ITEMS - 1) % N_GROUPS
    scan_b = lambda s: jnp.maximum(s - 1, 0) // N_GROUPS
    scan_g = lambda s: jnp.maximum(s - 1, 0) % N_GROUPS
    return pl.pallas_call(
        _ssd_kernel,
        grid=(N_ITEMS + 1,),
        in_specs=[
            pl.BlockSpec((1, SEQ, D_MODEL), lambda s: (proj_b(s), 0, 0),
                         pipeline_mode=pl.Buffered(1)),
            pl.BlockSpec((D_MODEL, GROUP_INNER), lambda s: (0, proj_g(s))),
            pl.BlockSpec((D_MODEL, GROUP_INNER), lambda s: (0, x_blk0 + proj_g(s))),
            pl.BlockSpec((D_MODEL, D_STATE), lambda s: (0, b_blk0 + proj_g(s))),
            pl.BlockSpec((D_MODEL, D_STATE), lambda s: (0, c_blk0 + proj_g(s))),
            pl.BlockSpec((D_MODEL, LANES), lambda s: (0, 0)),
            pl.BlockSpec((CONV_WIDTH, GROUP_INNER), lambda s: (0, scan_g(s))),
            pl.BlockSpec((CONV_WIDTH, D_STATE), lambda s: (0, cb_blk0 + scan_g(s))),
            pl.BlockSpec((CONV_WIDTH, D_STATE), lambda s: (0, cc_blk0 + scan_g(s))),
            pl.BlockSpec((1, GROUP_INNER), lambda s: (0, scan_g(s))),
            pl.BlockSpec((1, D_STATE), lambda s: (0, cb_blk0 + scan_g(s))),
            pl.BlockSpec((1, D_STATE), lambda s: (0, cc_blk0 + scan_g(s))),
            pl.BlockSpec((1, LANES), lambda s: (0, 0)),
            pl.BlockSpec((1, LANES), lambda s: (0, 0)),
            pl.BlockSpec((1, GROUP_INNER), lambda s: (0, scan_g(s))),
            pl.BlockSpec((1, GROUP_INNER), lambda s: (0, scan_g(s))),
        ],
        out_specs=[
            pl.BlockSpec((1, SEQ, GROUP_INNER), lambda s: (scan_b(s), 0, scan_g(s))),
            pl.BlockSpec((1, SEQ, 1), lambda s: (scan_b(s), 0, 0)),
        ],
        out_shape=[
            jax.ShapeDtypeStruct((BATCH, SEQ, D_INNER), BF16),
            jax.ShapeDtypeStruct((BATCH, SEQ, 1), F32),
        ],
        scratch_shapes=[
            pltpu.VMEM((SEQ, GROUP_INNER), F32),
            pltpu.VMEM((SEQ, GROUP_INNER), F32),
            pltpu.VMEM((CONV_BLOCKS, SEQ + CONV_PAD, LANES), F32),
            pltpu.VMEM((CONV_BLOCKS, SEQ + CONV_PAD, LANES), F32),
            pltpu.VMEM((D_MODEL, 2 * D_STATE), BF16),
            pltpu.VMEM((2, SEQ // CHUNK, LANES, CHUNK), F32),
            pltpu.VMEM((2, SEQ // CHUNK, LANES, CHUNK), F32),
            pltpu.VMEM((D_STATE, GROUP_INNER), F32),
        ],
        compiler_params=pltpu.CompilerParams(
            dimension_semantics=("arbitrary",),
            vmem_limit_bytes=VMEM_LIMIT),
        name="ssd_scan",
    )(xb, in_w_bf, in_w_bf, in_w_bf, in_w_bf, wdt_bf, cw, cw, cw, cb, cb, cb, dtb, alog, dsk, nw)


def _proj_ln_kernel(u_ref, w_ref, ssq_ref, x_ref, g_ref, b_ref, o_ref):
    acc = jnp.dot(u_ref[...], w_ref[...], preferred_element_type=F32)
    r = lax.rsqrt(ssq_ref[...] * (1.0 / D_INNER) + RMS_EPS)
    y = ALPHA * x_ref[...] + acc * r
    o_ref[...] = _layer_norm(y, g_ref[...], b_ref[...])


def _proj_ln(u, w_bf, ssq, x, g, b, *, tm=512):
    n_tok = x.shape[0]
    return pl.pallas_call(
        _proj_ln_kernel,
        grid=(n_tok // tm,),
        in_specs=[
            pl.BlockSpec((tm, D_INNER), lambda i: (i, 0)),
            pl.BlockSpec((D_INNER, D_MODEL), lambda i: (0, 0), pipeline_mode=pl.Buffered(1)),
            pl.BlockSpec((tm, 1), lambda i: (i, 0)),
            pl.BlockSpec((tm, D_MODEL), lambda i: (i, 0)),
            pl.BlockSpec((1, D_MODEL), lambda i: (0, 0)),
            pl.BlockSpec((1, D_MODEL), lambda i: (0, 0)),
        ],
        out_specs=pl.BlockSpec((tm, D_MODEL), lambda i: (i, 0)),
        out_shape=jax.ShapeDtypeStruct((n_tok, D_MODEL), F32),
        compiler_params=pltpu.CompilerParams(
            dimension_semantics=("parallel",),
            vmem_limit_bytes=VMEM_LIMIT),
        name="proj_ln",
    )(u, w_bf, ssq, x, g, b)


def _pad_heads(a):
    pad = [(0, 0)] * (a.ndim - 1) + [(0, LANES - N_HEADS)]
    return jnp.pad(a, pad)


def kernel(x, p, pool_w, pool_scale, ssm_in_w, ssm_conv_w, ssm_conv_b, ssm_dt_bias,
           ssm_a_log, ssm_d, ssm_norm_w, ssm_out_w, mlp_w1, mlp_w2, ln_g, ln_b,
           ple_w, ple_gate_w):
    n_tok = BATCH * SEQ
    row = lambda a: a.reshape(1, -1)

    x1 = _pool_ln(x, pool_w[0].astype(BF16), row(pool_scale[0]),
                  row(ln_g[0, 0]), row(ln_b[0, 0]))
    w1_bf, w2_bf = mlp_w1.astype(BF16), mlp_w2.astype(BF16)
    plew_bf, gatew_bf = ple_w.astype(BF16), ple_gate_w.astype(BF16)
    p_tok = p.reshape(DEPTH, n_tok, PLE_DIM)
    x2, x2b = _mlp_ln_ple(0, x1.reshape(n_tok, D_MODEL), w1_bf, w2_bf,
                          row(ln_g[0, 1]), row(ln_b[0, 1]), p_tok, plew_bf, gatew_bf)

    in_w_bf = ssm_in_w[0].astype(BF16)
    wdt = _pad_heads(in_w_bf[:, 2 * D_INNER + 2 * N_GROUPS * D_STATE:])
    yg, ssq = _ssd_scan(
        x2b.reshape(BATCH, SEQ, D_MODEL), in_w_bf, wdt, ssm_conv_w[0], row(ssm_conv_b[0]),
        row(_pad_heads(ssm_dt_bias[0])), row(_pad_heads(ssm_a_log[0])),
        row(jnp.repeat(ssm_d[0], HEAD_DIM)), row(ssm_norm_w[0]))
    x3 = _proj_ln(yg.reshape(n_tok, D_INNER), ssm_out_w[0].astype(BF16),
                  ssq.reshape(n_tok, 1), x2, row(ln_g[1, 0]), row(ln_b[1, 0]))
    x4, _ = _mlp_ln_ple(1, x3, w1_bf, w2_bf,
                        row(ln_g[1, 1]), row(ln_b[1, 1]), p_tok, plew_bf, gatew_bf)
    return x4.reshape(BATCH, SEQ, D_MODEL)
```

```python
import functools

import jax
import jax.numpy as jnp
from jax import lax
from jax.experimental import pallas as pl
from jax.experimental.pallas import tpu as pltpu

F32 = jnp.float32
BF16 = jnp.bfloat16

D_MODEL = 2048
BATCH = 8
SEQ = 2048
DEPTH = 2
ALPHA = (2.0 * DEPTH) ** 0.25
LN_EPS = 1e-5

POOL_WINDOWS = (2, 4, 8, 16)
POOL_GROUP_DIM = D_MODEL // len(POOL_WINDOWS)
POOL_HALO = 16

D_INNER = 2 * D_MODEL
HEAD_DIM = 64
N_HEADS = D_INNER // HEAD_DIM
N_GROUPS = 8
HEADS_PER_GROUP = N_HEADS // N_GROUPS
D_STATE = 128
CONV_WIDTH = 4
CHUNK = 128
GROUP_INNER = D_INNER // N_GROUPS
GROUP_CONV = GROUP_INNER + 2 * D_STATE
RMS_EPS = 1e-5
N_ITEMS = BATCH * N_GROUPS

D_FF = 4 * D_MODEL
PLE_DIM = 256

LANES = 128
SUBLANES = 8
CONV_PAD = SUBLANES
CONV_BLOCKS = GROUP_CONV // LANES
PROJ_COLS = 256
CHUNKS_PER_ITER = 2
PROJ_ROWS = CHUNKS_PER_ITER * CHUNK
PIECE_AT_FILL = (2, 4, 6, 8, 10)
VMEM_LIMIT = 60 * 1024 * 1024


def _silu(x):
    h = 0.5 * x
    return h + h * jnp.tanh(h)


def _layer_norm(y, g, b):
    mu = jnp.mean(y, axis=-1, keepdims=True)
    d = y - mu
    var = jnp.mean(d * d, axis=-1, keepdims=True)
    return d * lax.rsqrt(var + LN_EPS) * g + b


def _pool_ln_kernel(x_ref, halo_ref, w_ref, scale_ref, g_ref, b_ref, o_ref, *, ts):
    i = pl.program_id(1)
    halo_on = (i > 0).astype(F32)
    pos = lax.broadcasted_iota(jnp.int32, (ts, 1), 0) + i * ts
    for gi, win in enumerate(POOL_WINDOWS):
        sl = slice(gi * POOL_GROUP_DIM, (gi + 1) * POOL_GROUP_DIM)
        xg = x_ref[0, :, sl]
        s = jnp.concatenate([halo_ref[0, :, sl] * halo_on, xg], axis=0)
        k = 1
        while k < win:
            s = s + pltpu.roll(s, k, axis=0)
            k *= 2
        cnt = jnp.minimum(pos + 1, win).astype(F32)
        pooled = s[POOL_HALO:] / cnt - xg
        h = jnp.dot(pooled.astype(BF16), w_ref[gi], preferred_element_type=F32)
        o_ref[0, :, sl] = ALPHA * xg + h * scale_ref[:, sl]
    o_ref[0] = _layer_norm(o_ref[0], g_ref[...], b_ref[...])


def _pool_ln(x, w_bf, scale, g, b, *, ts=256):
    nt = SEQ // ts
    hb = ts // POOL_HALO
    return pl.pallas_call(
        functools.partial(_pool_ln_kernel, ts=ts),
        grid=(BATCH, nt),
        in_specs=[
            pl.BlockSpec((1, ts, D_MODEL), lambda b_, i: (b_, i, 0)),
            pl.BlockSpec((1, POOL_HALO, D_MODEL),
                         lambda b_, i: (b_, jnp.maximum(i * hb - 1, 0), 0)),
            pl.BlockSpec((len(POOL_WINDOWS), POOL_GROUP_DIM, POOL_GROUP_DIM),
                         lambda b_, i: (0, 0, 0)),
            pl.BlockSpec((1, D_MODEL), lambda b_, i: (0, 0)),
            pl.BlockSpec((1, D_MODEL), lambda b_, i: (0, 0)),
            pl.BlockSpec((1, D_MODEL), lambda b_, i: (0, 0)),
        ],
        out_specs=pl.BlockSpec((1, ts, D_MODEL), lambda b_, i: (b_, i, 0)),
        out_shape=jax.ShapeDtypeStruct((BATCH, SEQ, D_MODEL), F32),
        compiler_params=pltpu.CompilerParams(
            dimension_semantics=("parallel", "arbitrary"),
            vmem_limit_bytes=VMEM_LIMIT),
        name="pool_ln",
    )(x, x, w_bf, scale, g, b)


def _mlp_kernel(x_ref, w1_ref, w2_ref, g_ref, b_ref, p_ref, plew_ref, gatew_ref,
                o_ref, ob_ref, xb_s, *, n_chunks):
    j = pl.program_id(1)

    @pl.when(j == 0)
    def _():
        x = x_ref[...]
        xb_s[...] = x.astype(BF16)
        o_ref[...] = ALPHA * x

    h = jnp.dot(xb_s[...], w1_ref[...], preferred_element_type=F32)
    h = jnp.maximum(h, 0.0)
    h = h * h
    o_ref[...] += jnp.dot(h.astype(BF16), w2_ref[...], preferred_element_type=F32)

    @pl.when(j == n_chunks - 1)
    def _():
        x2 = _layer_norm(o_ref[...], g_ref[...], b_ref[...])
        gate = jax.nn.sigmoid(
            jnp.dot(x2.astype(BF16), gatew_ref[...], preferred_element_type=F32))
        pe = jnp.dot(p_ref[...].astype(BF16), plew_ref[...], preferred_element_type=F32)
        out = x2 + gate * pe
        o_ref[...] = out
        ob_ref[...] = out.astype(BF16)


def _mlp_ln_ple(layer, x, w1_bf, w2_bf, g, b, p, plew_bf, gatew_bf, *, tm=512, tf=1024):
    n_tok = x.shape[0]
    n_chunks = D_FF // tf
    return pl.pallas_call(
        functools.partial(_mlp_kernel, n_chunks=n_chunks),
        grid=(n_tok // tm, n_chunks),
        in_specs=[
            pl.BlockSpec((tm, D_MODEL), lambda i, j: (i, 0)),
            pl.BlockSpec((None, D_MODEL, tf), lambda i, j: (layer, 0, j)),
            pl.BlockSpec((None, tf, D_MODEL), lambda i, j: (layer, j, 0)),
            pl.BlockSpec((1, D_MODEL), lambda i, j: (0, 0)),
            pl.BlockSpec((1, D_MODEL), lambda i, j: (0, 0)),
            pl.BlockSpec((None, tm, PLE_DIM), lambda i, j: (layer, i, 0)),
            pl.BlockSpec((None, PLE_DIM, D_MODEL), lambda i, j: (layer, 0, 0),
                         pipeline_mode=pl.Buffered(1)),
            pl.BlockSpec((None, D_MODEL, D_MODEL), lambda i, j: (layer, 0, 0),
                         pipeline_mode=pl.Buffered(1)),
        ],
        out_specs=[
            pl.BlockSpec((tm, D_MODEL), lambda i, j: (i, 0)),
            pl.BlockSpec((tm, D_MODEL), lambda i, j: (i, 0)),
        ],
        out_shape=[
            jax.ShapeDtypeStruct((n_tok, D_MODEL), F32),
            jax.ShapeDtypeStruct((n_tok, D_MODEL), BF16),
        ],
        scratch_shapes=[pltpu.VMEM((tm, D_MODEL), BF16)],
        compiler_params=pltpu.CompilerParams(
            dimension_semantics=("parallel", "arbitrary"),
            vmem_limit_bytes=VMEM_LIMIT),
        name="mlp_ln_ple",
    )(x, w1_bf, w2_bf, g, b, p, plew_bf, gatew_bf)


def _ssd_kernel(xb_ref, wz_ref, wx_ref, wb_ref, wc_ref, wdt_ref, cwx_ref, cwb_ref, cwc_ref,
                cbx_ref, cbb_ref, cbc_ref, dtb_ref, alog_ref, dsk_ref, nw_ref,
                yg_ref, ssq_ref, z0_s, z1_s, xbc0_s, xbc1_s, wbc_s, dtt_s, acst_s, state_s):
    s = pl.program_id(0)
    t = jnp.minimum(s, N_ITEMS - 1)
    t_bslot = (t // N_GROUPS) & 1
    sc = jnp.maximum(s - 1, 0)
    g = sc % N_GROUPS
    s_bslot = (sc // N_GROUPS) & 1
    slots = ((z0_s, xbc0_s), (z1_s, xbc1_s))

    @pl.when((t % N_GROUPS == 0) & (s < N_ITEMS))
    def _():
        dt_raw = jnp.dot(xb_ref[0], wdt_ref[...], preferred_element_type=F32) + dtb_ref[...]
        dt = jnp.maximum(dt_raw, 0.0) + jnp.log1p(jnp.exp(-jnp.abs(dt_raw)))
        a = dt * (-jnp.exp(alog_ref[...]))
        in_chunk = lax.broadcasted_iota(jnp.int32, (SEQ, 1), 0) & (CHUNK - 1)
        k = 1
        while k < CHUNK:
            a = a + jnp.where(in_chunk >= k, pltpu.roll(a, k, axis=0), 0.0)
            k *= 2
        for c in range(SEQ // CHUNK):
            dtt_s[t_bslot, c] = dt[c * CHUNK:(c + 1) * CHUNK, :].T
            acst_s[t_bslot, c] = a[c * CHUNK:(c + 1) * CHUNK, :].T

    @pl.when(s < N_ITEMS)
    def _():
        wbc_s[:, :D_STATE] = wb_ref[...]
        wbc_s[:, D_STATE:] = wc_ref[...]

    @pl.when(s == 0)
    def _():
        for _, xbc_s in slots:
            for j in range(CONV_BLOCKS):
                xbc_s[j, 0:CONV_PAD, :] = jnp.zeros((CONV_PAD, LANES), F32)

    def proj_pieces(c, z_s, xbc_s):
        r0 = pl.multiple_of(c * PROJ_ROWS, PROJ_ROWS)
        xr = xb_ref[0, pl.ds(r0, PROJ_ROWS), :]

        def z_piece(n):
            cols = slice(n * PROJ_COLS, (n + 1) * PROJ_COLS)
            z_s[pl.ds(r0, PROJ_ROWS), cols] = jnp.dot(
                xr, wz_ref[:, cols], preferred_element_type=F32)

        def xbc_piece(n):
            if n < GROUP_INNER // PROJ_COLS:
                w = wx_ref[:, n * PROJ_COLS:(n + 1) * PROJ_COLS]
            else:
                w = wbc_s[...]
            res = jnp.dot(xr, w, preferred_element_type=F32)
            for jj in range(PROJ_COLS // LANES):
                j = n * (PROJ_COLS // LANES) + jj
                xbc_s[j, pl.ds(r0 + CONV_PAD, PROJ_ROWS), :] = res[:, jj * LANES:(jj + 1) * LANES]

        return ([functools.partial(z_piece, n) for n in range(GROUP_INNER // PROJ_COLS)]
                + [functools.partial(xbc_piece, n) for n in range(GROUP_CONV // PROJ_COLS)])

    lo = lax.broadcasted_iota(jnp.int32, (1, LANES), 1) < HEAD_DIM
    ri = lax.broadcasted_iota(jnp.int32, (CHUNK, CHUNK), 0)
    ci = lax.broadcasted_iota(jnp.int32, (CHUNK, CHUNK), 1)
    causal = ri >= ci


    def conv_chunk(c, xbc_s, fill):
        cw = jnp.concatenate([cwx_ref[...], cwb_ref[...], cwc_ref[...]], axis=1)
        cb = jnp.concatenate([cbx_ref[...], cbb_ref[...], cbc_ref[...]], axis=1)
        r0 = pl.multiple_of(c * CHUNK, CHUNK)
        u = []
        for j in range(CONV_BLOCKS):
            bl = slice(j * LANES, (j + 1) * LANES)
            acc = cb[:, bl]
            for k in range(CONV_WIDTH):
                off = CONV_PAD - (CONV_WIDTH - 1) + k
                acc = acc + xbc_s[j, pl.ds(r0 + off, CHUNK), :] * cw[k:k + 1, bl]
            u.append(_silu(acc))
            if j % 2 == 1:
                fill()
        return u

    def head_pair_columns(rows):
        stacked = jnp.concatenate(
            [jnp.broadcast_to(r, (HEAD_DIM, CHUNK)) for r in rows], axis=0)
        return stacked.T

    def scan_chunk(c, u, z_s, fill):
        dsk = dsk_ref[...]
        nw = nw_ref[...]
        r0 = pl.multiple_of(c * CHUNK, CHUNK)
        rows = pl.ds(r0, CHUNK)
        bm = u[CONV_BLOCKS - 2]
        cm = u[CONV_BLOCKS - 1]
        bm_bf = bm.astype(BF16)
        cm_bf = cm.astype(BF16)
        cbm = lax.dot_general(cm_bf, bm_bf, (((1,), (1,)), ((), ())),
                              preferred_element_type=F32)
        state = state_s[...]
        y_off = jnp.dot(cm_bf, state.astype(BF16), preferred_element_type=F32)

        y_blocks, xdec_blocks, cdec_blocks = [], [], []
        for kb in range(HEADS_PER_GROUP // 2):
            bl = slice(kb * LANES, (kb + 1) * LANES)
            heads = [g * HEADS_PER_GROUP + 2 * kb + hh for hh in range(2)]
            a_rows = [acst_s[s_bslot, c, pl.ds(h, 1), :] for h in heads]
            d_rows = [dtt_s[s_bslot, c, pl.ds(h, 1), :] for h in heads]
            a_blk = head_pair_columns(a_rows)
            dt_blk = head_pair_columns(d_rows)
            a_end = a_blk[CHUNK - 1:CHUNK, :]
            xs_blk = u[kb]
            xdt = xs_blk * dt_blk
            ms = []
            for hh in range(2):
                a_row = a_rows[hh]
                a_col = jnp.broadcast_to(a_row, (CHUNK, CHUNK)).T
                seg = a_col - a_row
                lmat = jnp.exp(jnp.where(causal, seg, -jnp.inf))
                ms.append((cbm * lmat).astype(BF16))
            lhs = jnp.concatenate(ms, axis=1)
            rhs = jnp.concatenate([jnp.where(lo, xdt, 0.0), jnp.where(lo, 0.0, xdt)],
                                  axis=0).astype(BF16)
            y_diag = jnp.dot(lhs, rhs, preferred_element_type=F32)
            y_blocks.append(y_diag + y_off[:, bl] * jnp.exp(a_blk) + xs_blk * dsk[:, bl])
            xdec_blocks.append(xdt * jnp.exp(a_end - a_blk))
            cdec_blocks.append(jnp.exp(a_end))
            if kb % 2 == 1:
                fill()
        y = jnp.concatenate(y_blocks, axis=1)
        xdec = jnp.concatenate(xdec_blocks, axis=1).astype(BF16)
        cdec = jnp.concatenate(cdec_blocks, axis=1)
        st_new = jnp.dot(bm.T.astype(BF16), xdec, preferred_element_type=F32)
        state_s[...] = state * cdec + st_new

        zc = z_s[rows, :]
        v = y * _silu(zc)
        ssq_ref[0, rows, :] += jnp.sum(v * v, axis=1, keepdims=True)
        yg_ref[0, rows, :] = (v * nw).astype(BF16)

    def run(scan_slot, proj_slot):
        def body(c, carry):
            pieces = iter(proj_pieces(c, *slots[proj_slot]) if proj_slot is not None else ())

            calls = [0]

            def fill():
                calls[0] += 1
                if calls[0] in PIECE_AT_FILL:
                    piece = next(pieces, None)
                    if piece is not None:
                        piece()

            if scan_slot is not None:
                z_s, xbc_s = slots[scan_slot]
                chunks = [c * CHUNKS_PER_ITER + cc for cc in range(CHUNKS_PER_ITER)]
                us = [conv_chunk(ch, xbc_s, fill) for ch in chunks]
                for ch, u in zip(chunks, us):
                    scan_chunk(ch, u, z_s, fill)
            for piece in pieces:
                piece()
            return carry
        lax.fori_loop(0, SEQ // PROJ_ROWS, body, 0)

    @pl.when(s == 0)
    def _():
        run(None, 0)

    @pl.when(s > 0)
    def _():
        state_s[...] = jnp.zeros_like(state_s)

        @pl.when(g == 0)
        def _():
            ssq_ref[...] = jnp.zeros_like(ssq_ref)

    for parity in range(2):
        @pl.when((s > 0) & (s < N_ITEMS) & (s % 2 == parity))
        def _():
            run(1 - parity, parity)

    @pl.when(s == N_ITEMS)
    def _():
        run((N_ITEMS - 1) % 2, None)


def _ssd_scan(xb, in_w_bf, wdt_bf, cw, cb, dtb, alog, dsk, nw):
    x_blk0 = D_INNER // GROUP_INNER
    b_blk0 = 2 * D_INNER // D_STATE
    c_blk0 = b_blk0 + N_GROUPS
    cb_blk0 = D_INNER // D_STATE
    cc_blk0 = cb_blk0 + N_GROUPS
    proj_b = lambda s: jnp.minimum(s, N_ITEMS - 1) // N_GROUPS
    proj_g = lambda s: jnp.minimum(s, N_ITEMS - 1) % N_GROUPS
    scan_b = lambda s: jnp.maximum(s - 1, 0) // N_GROUPS
    scan_g = lambda s: jnp.maximum(s - 1, 0) % N_GROUPS
    return pl.pallas_call(
        _ssd_kernel,
        grid=(N_ITEMS + 1,),
        in_specs=[
            pl.BlockSpec((1, SEQ, D_MODEL), lambda s: (proj_b(s), 0, 0),
                         pipeline_mode=pl.Buffered(1)),
            pl.BlockSpec((D_MODEL, GROUP_INNER), lambda s: (0, proj_g(s))),
            pl.BlockSpec((D_MODEL, GROUP_INNER), lambda s: (0, x_blk0 + proj_g(s))),
            pl.BlockSpec((D_MODEL, D_STATE), lambda s: (0, b_blk0 + proj_g(s))),
            pl.BlockSpec((D_MODEL, D_STATE), lambda s: (0, c_blk0 + proj_g(s))),
            pl.BlockSpec((D_MODEL, LANES), lambda s: (0, 0)),
            pl.BlockSpec((CONV_WIDTH, GROUP_INNER), lambda s: (0, scan_g(s))),
            pl.BlockSpec((CONV_WIDTH, D_STATE), lambda s: (0, cb_blk0 + scan_g(s))),
            pl.BlockSpec((CONV_WIDTH, D_STATE), lambda s: (0, cc_blk0 + scan_g(s))),
            pl.BlockSpec((1, GROUP_INNER), lambda s: (0, scan_g(s))),
            pl.BlockSpec((1, D_STATE), lambda s: (0, cb_blk0 + scan_g(s))),
            pl.BlockSpec((1, D_STATE), lambda s: (0, cc_blk0 + scan_g(s))),
            pl.BlockSpec((1, LANES), lambda s: (0, 0)),
            pl.BlockSpec((1, LANES), lambda s: (0, 0)),
            pl.BlockSpec((1, GROUP_INNER), lambda s: (0, scan_g(s))),
            pl.BlockSpec((1, GROUP_INNER), lambda s: (0, scan_g(s))),
        ],
        out_specs=[
            pl.BlockSpec((1, SEQ, GROUP_INNER), lambda s: (scan_b(s), 0, scan_g(s))),
            pl.BlockSpec((1, SEQ, 1), lambda s: (scan_b(s), 0, 0)),
        ],
        out_shape=[
            jax.ShapeDtypeStruct((BATCH, SEQ, D_INNER), BF16),
            jax.ShapeDtypeStruct((BATCH, SEQ, 1), F32),
        ],
        scratch_shapes=[
            pltpu.VMEM((SEQ, GROUP_INNER), F32),
            pltpu.VMEM((SEQ, GROUP_INNER), F32),
            pltpu.VMEM((CONV_BLOCKS, SEQ + CONV_PAD, LANES), F32),
            pltpu.VMEM((CONV_BLOCKS, SEQ + CONV_PAD, LANES), F32),
            pltpu.VMEM((D_MODEL, 2 * D_STATE), BF16),
            pltpu.VMEM((2, SEQ // CHUNK, LANES, CHUNK), F32),
            pltpu.VMEM((2, SEQ // CHUNK, LANES, CHUNK), F32),
            pltpu.VMEM((D_STATE, GROUP_INNER), F32),
        ],
        compiler_params=pltpu.CompilerParams(
            dimension_semantics=("arbitrary",),
            vmem_limit_bytes=VMEM_LIMIT),
        name="ssd_scan",
    )(xb, in_w_bf, in_w_bf, in_w_bf, in_w_bf, wdt_bf, cw, cw, cw, cb, cb, cb, dtb, alog, dsk, nw)


def _proj_ln_kernel(u_ref, w_ref, ssq_ref, x_ref, g_ref, b_ref, o_ref):
    acc = jnp.dot(u_ref[...], w_ref[...], preferred_element_type=F32)
    r = lax.rsqrt(ssq_ref[...] * (1.0 / D_INNER) + RMS_EPS)
    y = ALPHA * x_ref[...] + acc * r
    o_ref[...] = _layer_norm(y, g_ref[...], b_ref[...])


def _proj_ln(u, w_bf, ssq, x, g, b, *, tm=512):
    n_tok = x.shape[0]
    return pl.pallas_call(
        _proj_ln_kernel,
        grid=(n_tok // tm,),
        in_specs=[
            pl.BlockSpec((tm, D_INNER), lambda i: (i, 0)),
            pl.BlockSpec((D_INNER, D_MODEL), lambda i: (0, 0), pipeline_mode=pl.Buffered(1)),
            pl.BlockSpec((tm, 1), lambda i: (i, 0)),
            pl.BlockSpec((tm, D_MODEL), lambda i: (i, 0)),
            pl.BlockSpec((1, D_MODEL), lambda i: (0, 0)),
            pl.BlockSpec((1, D_MODEL), lambda i: (0, 0)),
        ],
        out_specs=pl.BlockSpec((tm, D_MODEL), lambda i: (i, 0)),
        out_shape=jax.ShapeDtypeStruct((n_tok, D_MODEL), F32),
        compiler_params=pltpu.CompilerParams(
            dimension_semantics=("parallel",),
            vmem_limit_bytes=VMEM_LIMIT),
        name="proj_ln",
    )(u, w_bf, ssq, x, g, b)


def _pad_heads(a):
    pad = [(0, 0)] * (a.ndim - 1) + [(0, LANES - N_HEADS)]
    return jnp.pad(a, pad)


def kernel(x, p, pool_w, pool_scale, ssm_in_w, ssm_conv_w, ssm_conv_b, ssm_dt_bias,
           ssm_a_log, ssm_d, ssm_norm_w, ssm_out_w, mlp_w1, mlp_w2, ln_g, ln_b,
           ple_w, ple_gate_w):
    n_tok = BATCH * SEQ
    row = lambda a: a.reshape(1, -1)

    x1 = _pool_ln(x, pool_w[0].astype(BF16), row(pool_scale[0]),
                  row(ln_g[0, 0]), row(ln_b[0, 0]))
    w1_bf, w2_bf = mlp_w1.astype(BF16), mlp_w2.astype(BF16)
    plew_bf, gatew_bf = ple_w.astype(BF16), ple_gate_w.astype(BF16)
    p_tok = p.reshape(DEPTH, n_tok, PLE_DIM)
    x2, x2b = _mlp_ln_ple(0, x1.reshape(n_tok, D_MODEL), w1_bf, w2_bf,
                          row(ln_g[0, 1]), row(ln_b[0, 1]), p_tok, plew_bf, gatew_bf)

    n_zxbc = 2 * D_INNER + 2 * N_GROUPS * D_STATE
    in_w_bf = ssm_in_w[0, :, :n_zxbc].astype(BF16)
    wdt = _pad_heads(ssm_in_w[0, :, n_zxbc:]).astype(BF16)
    yg, ssq = _ssd_scan(
        x2b.reshape(BATCH, SEQ, D_MODEL), in_w_bf, wdt, ssm_conv_w[0], row(ssm_conv_b[0]),
        row(_pad_heads(ssm_dt_bias[0])), row(_pad_heads(ssm_a_log[0])),
        row(jnp.repeat(ssm_d[0], HEAD_DIM)), row(ssm_norm_w[0]))
    x3 = _proj_ln(yg.reshape(n_tok, D_INNER), ssm_out_w[0].astype(BF16),
                  ssq.reshape(n_tok, 1), x2, row(ln_g[1, 0]), row(ln_b[1, 0]))
    x4, _ = _mlp_ln_ple(1, x3, w1_bf, w2_bf,
                        row(ln_g[1, 1]), row(ln_b[1, 1]), p_tok, plew_bf, gatew_bf)
    return x4.reshape(BATCH, SEQ, D_MODEL)
```

```python
import functools

import jax
import jax.numpy as jnp
from jax import lax
from jax.experimental import pallas as pl
from jax.experimental.pallas import tpu as pltpu

F32 = jnp.float32
BF16 = jnp.bfloat16

D_MODEL = 2048
BATCH = 8
SEQ = 2048
DEPTH = 2
ALPHA = (2.0 * DEPTH) ** 0.25
LN_EPS = 1e-5

POOL_WINDOWS = (2, 4, 8, 16)
POOL_GROUP_DIM = D_MODEL // len(POOL_WINDOWS)
POOL_HALO = 16

D_INNER = 2 * D_MODEL
HEAD_DIM = 64
N_HEADS = D_INNER // HEAD_DIM
N_GROUPS = 8
HEADS_PER_GROUP = N_HEADS // N_GROUPS
D_STATE = 128
CONV_WIDTH = 4
CHUNK = 128
GROUP_INNER = D_INNER // N_GROUPS
GROUP_CONV = GROUP_INNER + 2 * D_STATE
RMS_EPS = 1e-5
N_ITEMS = BATCH * N_GROUPS

D_FF = 4 * D_MODEL
PLE_DIM = 256

LANES = 128
SUBLANES = 8
CONV_PAD = SUBLANES
CONV_BLOCKS = GROUP_CONV // LANES
PROJ_COLS = 256
CHUNKS_PER_ITER = 2
PROJ_ROWS = CHUNKS_PER_ITER * CHUNK
PIECE_AT_FILL = (2, 4, 6, 8, 10)
VMEM_LIMIT = 60 * 1024 * 1024


def _silu(x):
    h = 0.5 * x
    return h + h * jnp.tanh(h)


def _layer_norm(y, g, b):
    mu = jnp.mean(y, axis=-1, keepdims=True)
    d = y - mu
    var = jnp.mean(d * d, axis=-1, keepdims=True)
    return d * lax.rsqrt(var + LN_EPS) * g + b


def _pool_ln_kernel(x_ref, halo_ref, w_ref, scale_ref, g_ref, b_ref, o_ref, *, ts):
    i = pl.program_id(1)
    halo_on = (i > 0).astype(F32)
    pos = lax.broadcasted_iota(jnp.int32, (ts, 1), 0) + i * ts
    for gi, win in enumerate(POOL_WINDOWS):
        sl = slice(gi * POOL_GROUP_DIM, (gi + 1) * POOL_GROUP_DIM)
        xg = x_ref[0, :, sl]
        s = jnp.concatenate([halo_ref[0, :, sl] * halo_on, xg], axis=0)
        k = 1
        while k < win:
            s = s + pltpu.roll(s, k, axis=0)
            k *= 2
        cnt = jnp.minimum(pos + 1, win).astype(F32)
        pooled = s[POOL_HALO:] / cnt - xg
        h = jnp.dot(pooled.astype(BF16), w_ref[gi], preferred_element_type=F32)
        o_ref[0, :, sl] = ALPHA * xg + h * scale_ref[:, sl]
    o_ref[0] = _layer_norm(o_ref[0], g_ref[...], b_ref[...])


def _pool_ln(x, w_bf, scale, g, b, *, ts=256):
    nt = SEQ // ts
    hb = ts // POOL_HALO
    return pl.pallas_call(
        functools.partial(_pool_ln_kernel, ts=ts),
        grid=(BATCH, nt),
        in_specs=[
            pl.BlockSpec((1, ts, D_MODEL), lambda b_, i: (b_, i, 0)),
            pl.BlockSpec((1, POOL_HALO, D_MODEL),
                         lambda b_, i: (b_, jnp.maximum(i * hb - 1, 0), 0)),
            pl.BlockSpec((len(POOL_WINDOWS), POOL_GROUP_DIM, POOL_GROUP_DIM),
                         lambda b_, i: (0, 0, 0)),
            pl.BlockSpec((1, D_MODEL), lambda b_, i: (0, 0)),
            pl.BlockSpec((1, D_MODEL), lambda b_, i: (0, 0)),
            pl.BlockSpec((1, D_MODEL), lambda b_, i: (0, 0)),
        ],
        out_specs=pl.BlockSpec((1, ts, D_MODEL), lambda b_, i: (b_, i, 0)),
        out_shape=jax.ShapeDtypeStruct((BATCH, SEQ, D_MODEL), F32),
        compiler_params=pltpu.CompilerParams(
            dimension_semantics=("parallel", "arbitrary"),
            vmem_limit_bytes=VMEM_LIMIT),
        name="pool_ln",
    )(x, x, w_bf, scale, g, b)


def _mlp_kernel(x_ref, w1_hbm, w2_hbm, g_ref, b_ref, p_ref, plew_ref, gatew_ref,
                o_ref, ob_ref, xb_s, w1_buf, w2_buf, sem, *, layer, tf, n_chunks, n_tiles):
    i = pl.program_id(0)

    def chunk_copies(k, slot):
        cols = pl.ds(pl.multiple_of(k * tf, tf), tf)
        return (pltpu.make_async_copy(w1_hbm.at[layer, :, cols], w1_buf.at[slot], sem.at[0, slot]),
                pltpu.make_async_copy(w2_hbm.at[layer, cols, :], w2_buf.at[slot], sem.at[1, slot]))

    @pl.when(i == 0)
    def _():
        for cp in chunk_copies(0, 0):
            cp.start()

    x = x_ref[...]
    xb_s[...] = x.astype(BF16)
    o_ref[...] = ALPHA * x

    def body(k, carry):
        slot = k % 2
        for cp in chunk_copies(k, slot):
            cp.wait()
        k_next = jnp.where(k + 1 < n_chunks, k + 1, 0)

        @pl.when((k + 1 < n_chunks) | (i + 1 < n_tiles))
        def _():
            for cp in chunk_copies(k_next, 1 - slot):
                cp.start()

        h = jnp.dot(xb_s[...], w1_buf[slot], preferred_element_type=F32)
        h = jnp.maximum(h, 0.0)
        h = h * h
        o_ref[...] += jnp.dot(h.astype(BF16), w2_buf[slot], preferred_element_type=F32)
        return carry

    lax.fori_loop(0, n_chunks, body, 0)

    x2 = _layer_norm(o_ref[...], g_ref[...], b_ref[...])
    gate = jax.nn.sigmoid(
        jnp.dot(x2.astype(BF16), gatew_ref[...], preferred_element_type=F32))
    pe = jnp.dot(p_ref[...].astype(BF16), plew_ref[...], preferred_element_type=F32)
    out = x2 + gate * pe
    o_ref[...] = out
    ob_ref[...] = out.astype(BF16)


def _mlp_ln_ple(layer, x, w1_bf, w2_bf, g, b, p, plew_bf, gatew_bf, *, tm=512, tf=1024):
    n_tok = x.shape[0]
    n_chunks = D_FF // tf
    n_tiles = n_tok // tm
    assert n_chunks % 2 == 0
    return pl.pallas_call(
        functools.partial(_mlp_kernel, layer=layer, tf=tf, n_chunks=n_chunks, n_tiles=n_tiles),
        grid=(n_tiles,),
        in_specs=[
            pl.BlockSpec((tm, D_MODEL), lambda i: (i, 0)),
            pl.BlockSpec(memory_space=pl.ANY),
            pl.BlockSpec(memory_space=pl.ANY),
            pl.BlockSpec((1, D_MODEL), lambda i: (0, 0)),
            pl.BlockSpec((1, D_MODEL), lambda i: (0, 0)),
            pl.BlockSpec((None, tm, PLE_DIM), lambda i: (layer, i, 0)),
            pl.BlockSpec((None, PLE_DIM, D_MODEL), lambda i: (layer, 0, 0),
                         pipeline_mode=pl.Buffered(1)),
            pl.BlockSpec((None, D_MODEL, D_MODEL), lambda i: (layer, 0, 0),
                         pipeline_mode=pl.Buffered(1)),
        ],
        out_specs=[
            pl.BlockSpec((tm, D_MODEL), lambda i: (i, 0)),
            pl.BlockSpec((tm, D_MODEL), lambda i: (i, 0)),
        ],
        out_shape=[
            jax.ShapeDtypeStruct((n_tok, D_MODEL), F32),
            jax.ShapeDtypeStruct((n_tok, D_MODEL), BF16),
        ],
        scratch_shapes=[
            pltpu.VMEM((tm, D_MODEL), BF16),
            pltpu.VMEM((2, D_MODEL, tf), BF16),
            pltpu.VMEM((2, tf, D_MODEL), BF16),
            pltpu.SemaphoreType.DMA((2, 2)),
        ],
        compiler_params=pltpu.CompilerParams(
            dimension_semantics=("arbitrary",),
            vmem_limit_bytes=VMEM_LIMIT),
        name="mlp_ln_ple",
    )(x, w1_bf, w2_bf, g, b, p, plew_bf, gatew_bf)


def _ssd_kernel(xb_ref, wz_ref, wx_ref, wb_ref, wc_ref, wdt_ref, cwx_ref, cwb_ref, cwc_ref,
                cbx_ref, cbb_ref, cbc_ref, dtb_ref, alog_ref, dsk_ref, nw_ref,
                yg_ref, ssq_ref, z0_s, z1_s, xbc0_s, xbc1_s, wbc_s, dtt_s, acst_s, state_s):
    s = pl.program_id(0)
    t = jnp.minimum(s, N_ITEMS - 1)
    t_bslot = (t // N_GROUPS) & 1
    sc = jnp.maximum(s - 1, 0)
    g = sc % N_GROUPS
    s_bslot = (sc // N_GROUPS) & 1
    slots = ((z0_s, xbc0_s), (z1_s, xbc1_s))

    @pl.when((t % N_GROUPS == 0) & (s < N_ITEMS))
    def _():
        dt_raw = jnp.dot(xb_ref[0], wdt_ref[...], preferred_element_type=F32) + dtb_ref[...]
        dt = jnp.maximum(dt_raw, 0.0) + jnp.log1p(jnp.exp(-jnp.abs(dt_raw)))
        a = dt * (-jnp.exp(alog_ref[...]))
        in_chunk = lax.broadcasted_iota(jnp.int32, (SEQ, 1), 0) & (CHUNK - 1)
        k = 1
        while k < CHUNK:
            a = a + jnp.where(in_chunk >= k, pltpu.roll(a, k, axis=0), 0.0)
            k *= 2
        for c in range(SEQ // CHUNK):
            dtt_s[t_bslot, c] = dt[c * CHUNK:(c + 1) * CHUNK, :].T
            acst_s[t_bslot, c] = a[c * CHUNK:(c + 1) * CHUNK, :].T

    @pl.when(s < N_ITEMS)
    def _():
        wbc_s[:, :D_STATE] = wb_ref[...]
        wbc_s[:, D_STATE:] = wc_ref[...]

    @pl.when(s == 0)
    def _():
        for _, xbc_s in slots:
            for j in range(CONV_BLOCKS):
                xbc_s[j, 0:CONV_PAD, :] = jnp.zeros((CONV_PAD, LANES), F32)

    def proj_pieces(c, z_s, xbc_s):
        r0 = pl.multiple_of(c * PROJ_ROWS, PROJ_ROWS)
        xr = xb_ref[0, pl.ds(r0, PROJ_ROWS), :]

        def z_piece(n):
            cols = slice(n * PROJ_COLS, (n + 1) * PROJ_COLS)
            z_s[pl.ds(r0, PROJ_ROWS), cols] = jnp.dot(
                xr, wz_ref[:, cols], preferred_element_type=F32)

        def xbc_piece(n):
            if n < GROUP_INNER // PROJ_COLS:
                w = wx_ref[:, n * PROJ_COLS:(n + 1) * PROJ_COLS]
            else:
                w = wbc_s[...]
            res = jnp.dot(xr, w, preferred_element_type=F32)
            for jj in range(PROJ_COLS // LANES):
                j = n * (PROJ_COLS // LANES) + jj
                xbc_s[j, pl.ds(r0 + CONV_PAD, PROJ_ROWS), :] = res[:, jj * LANES:(jj + 1) * LANES]

        return ([functools.partial(z_piece, n) for n in range(GROUP_INNER // PROJ_COLS)]
                + [functools.partial(xbc_piece, n) for n in range(GROUP_CONV // PROJ_COLS)])

    lo = lax.broadcasted_iota(jnp.int32, (1, LANES), 1) < HEAD_DIM
    ri = lax.broadcasted_iota(jnp.int32, (CHUNK, CHUNK), 0)
    ci = lax.broadcasted_iota(jnp.int32, (CHUNK, CHUNK), 1)
    causal = ri >= ci


    def conv_chunk(c, xbc_s, fill):
        cw = jnp.concatenate([cwx_ref[...], cwb_ref[...], cwc_ref[...]], axis=1)
        cb = jnp.concatenate([cbx_ref[...], cbb_ref[...], cbc_ref[...]], axis=1)
        r0 = pl.multiple_of(c * CHUNK, CHUNK)
        u = []
        for j in range(CONV_BLOCKS):
            bl = slice(j * LANES, (j + 1) * LANES)
            acc = cb[:, bl]
            for k in range(CONV_WIDTH):
                off = CONV_PAD - (CONV_WIDTH - 1) + k
                acc = acc + xbc_s[j, pl.ds(r0 + off, CHUNK), :] * cw[k:k + 1, bl]
            u.append(_silu(acc))
            if j % 2 == 1:
                fill()
        return u

    def head_pair_columns(rows):
        stacked = jnp.concatenate(
            [jnp.broadcast_to(r, (HEAD_DIM, CHUNK)) for r in rows], axis=0)
        return stacked.T

    def scan_chunk(c, u, z_s, fill):
        dsk = dsk_ref[...]
        nw = nw_ref[...]
        r0 = pl.multiple_of(c * CHUNK, CHUNK)
        rows = pl.ds(r0, CHUNK)
        bm = u[CONV_BLOCKS - 2]
        cm = u[CONV_BLOCKS - 1]
        bm_bf = bm.astype(BF16)
        cm_bf = cm.astype(BF16)
        cbm = lax.dot_general(cm_bf, bm_bf, (((1,), (1,)), ((), ())),
                              preferred_element_type=F32)
        state = state_s[...]
        y_off = jnp.dot(cm_bf, state.astype(BF16), preferred_element_type=F32)

        prep = []
        for kb in range(HEADS_PER_GROUP // 2):
            heads = [g * HEADS_PER_GROUP + 2 * kb + hh for hh in range(2)]
            a_rows = [acst_s[s_bslot, c, pl.ds(h, 1), :] for h in heads]
            d_rows = [dtt_s[s_bslot, c, pl.ds(h, 1), :] for h in heads]
            a_blk = head_pair_columns(a_rows)
            dt_blk = head_pair_columns(d_rows)
            a_end = a_blk[CHUNK - 1:CHUNK, :]
            xdt = u[kb] * dt_blk
            lmats = []
            for a_row in a_rows:
                a_col = jnp.broadcast_to(a_row, (CHUNK, CHUNK)).T
                lmats.append(jnp.exp(jnp.where(causal, a_col - a_row, -jnp.inf)))
            rhs = jnp.concatenate([jnp.where(lo, xdt, 0.0), jnp.where(lo, 0.0, xdt)],
                                  axis=0).astype(BF16)
            prep.append((lmats, rhs, jnp.exp(a_blk), xdt * jnp.exp(a_end - a_blk),
                         jnp.exp(a_end)))
            if kb % 2 == 1:
                fill()
        xdec = jnp.concatenate([p[3] for p in prep], axis=1).astype(BF16)
        cdec = jnp.concatenate([p[4] for p in prep], axis=1)
        st_new = jnp.dot(bm.T.astype(BF16), xdec, preferred_element_type=F32)
        state_s[...] = state * cdec + st_new

        y_blocks = []
        for kb, (lmats, rhs, decay, _, _) in enumerate(prep):
            bl = slice(kb * LANES, (kb + 1) * LANES)
            lhs = jnp.concatenate([(cbm * lm).astype(BF16) for lm in lmats], axis=1)
            y_diag = jnp.dot(lhs, rhs, preferred_element_type=F32)
            y_blocks.append(y_diag + y_off[:, bl] * decay + u[kb] * dsk[:, bl])
        y = jnp.concatenate(y_blocks, axis=1)

        zc = z_s[rows, :]
        v = y * _silu(zc)
        ssq_ref[0, rows, :] += jnp.sum(v * v, axis=1, keepdims=True)
        yg_ref[0, rows, :] = (v * nw).astype(BF16)

    def run(scan_slot, proj_slot):
        def body(c, carry):
            pieces = iter(proj_pieces(c, *slots[proj_slot]) if proj_slot is not None else ())

            calls = [0]

            def fill():
                calls[0] += 1
                if calls[0] in PIECE_AT_FILL:
                    piece = next(pieces, None)
                    if piece is not None:
                        piece()

            if scan_slot is not None:
                z_s, xbc_s = slots[scan_slot]
                chunks = [c * CHUNKS_PER_ITER + cc for cc in range(CHUNKS_PER_ITER)]
                us = [conv_chunk(ch, xbc_s, fill) for ch in chunks]
                for ch, u in zip(chunks, us):
                    scan_chunk(ch, u, z_s, fill)
            for piece in pieces:
                piece()
            return carry
        lax.fori_loop(0, SEQ // PROJ_ROWS, body, 0)

    @pl.when(s == 0)
    def _():
        run(None, 0)

    @pl.when(s > 0)
    def _():
        state_s[...] = jnp.zeros_like(state_s)

        @pl.when(g == 0)
        def _():
            ssq_ref[...] = jnp.zeros_like(ssq_ref)

    for parity in range(2):
        @pl.when((s > 0) & (s < N_ITEMS) & (s % 2 == parity))
        def _():
            run(1 - parity, parity)

    @pl.when(s == N_ITEMS)
    def _():
        run((N_ITEMS - 1) % 2, None)


def _ssd_scan(xb, in_w_bf, wdt_bf, cw, cb, dtb, alog, dsk, nw):
    x_blk0 = D_INNER // GROUP_INNER
    b_blk0 = 2 * D_INNER // D_STATE
    c_blk0 = b_blk0 + N_GROUPS
    cb_blk0 = D_INNER // D_STATE
    cc_blk0 = cb_blk0 + N_GROUPS
    proj_b = lambda s: jnp.minimum(s, N_ITEMS - 1) // N_GROUPS
    proj_g = lambda s: jnp.minimum(s, N_ITEMS - 1) % N_GROUPS
    scan_b = lambda s: jnp.maximum(s - 1, 0) // N_GROUPS
    scan_g = lambda s: jnp.maximum(s - 1, 0) % N_GROUPS
    return pl.pallas_call(
        _ssd_kernel,
        grid=(N_ITEMS + 1,),
        in_specs=[
            pl.BlockSpec((1, SEQ, D_MODEL), lambda s: (proj_b(s), 0, 0),
                         pipeline_mode=pl.Buffered(1)),
            pl.BlockSpec((D_MODEL, GROUP_INNER), lambda s: (0, proj_g(s))),
            pl.BlockSpec((D_MODEL, GROUP_INNER), lambda s: (0, x_blk0 + proj_g(s))),
            pl.BlockSpec((D_MODEL, D_STATE), lambda s: (0, b_blk0 + proj_g(s))),
            pl.BlockSpec((D_MODEL, D_STATE), lambda s: (0, c_blk0 + proj_g(s))),
            pl.BlockSpec((D_MODEL, LANES), lambda s: (0, 0)),
            pl.BlockSpec((CONV_WIDTH, GROUP_INNER), lambda s: (0, scan_g(s))),
            pl.BlockSpec((CONV_WIDTH, D_STATE), lambda s: (0, cb_blk0 + scan_g(s))),
            pl.BlockSpec((CONV_WIDTH, D_STATE), lambda s: (0, cc_blk0 + scan_g(s))),
            pl.BlockSpec((1, GROUP_INNER), lambda s: (0, scan_g(s))),
            pl.BlockSpec((1, D_STATE), lambda s: (0, cb_blk0 + scan_g(s))),
            pl.BlockSpec((1, D_STATE), lambda s: (0, cc_blk0 + scan_g(s))),
            pl.BlockSpec((1, LANES), lambda s: (0, 0)),
            pl.BlockSpec((1, LANES), lambda s: (0, 0)),
            pl.BlockSpec((1, GROUP_INNER), lambda s: (0, scan_g(s))),
            pl.BlockSpec((1, GROUP_INNER), lambda s: (0, scan_g(s))),
        ],
        out_specs=[
            pl.BlockSpec((1, SEQ, GROUP_INNER), lambda s: (scan_b(s), 0, scan_g(s))),
            pl.BlockSpec((1, SEQ, 1), lambda s: (scan_b(s), 0, 0)),
        ],
        out_shape=[
            jax.ShapeDtypeStruct((BATCH, SEQ, D_INNER), BF16),
            jax.ShapeDtypeStruct((BATCH, SEQ, 1), F32),
        ],
        scratch_shapes=[
            pltpu.VMEM((SEQ, GROUP_INNER), F32),
            pltpu.VMEM((SEQ, GROUP_INNER), F32),
            pltpu.VMEM((CONV_BLOCKS, SEQ + CONV_PAD, LANES), F32),
            pltpu.VMEM((CONV_BLOCKS, SEQ + CONV_PAD, LANES), F32),
            pltpu.VMEM((D_MODEL, 2 * D_STATE), BF16),
            pltpu.VMEM((2, SEQ // CHUNK, LANES, CHUNK), F32),
            pltpu.VMEM((2, SEQ // CHUNK, LANES, CHUNK), F32),
            pltpu.VMEM((D_STATE, GROUP_INNER), F32),
        ],
        compiler_params=pltpu.CompilerParams(
            dimension_semantics=("arbitrary",),
            vmem_limit_bytes=VMEM_LIMIT),
        name="ssd_scan",
    )(xb, in_w_bf, in_w_bf, in_w_bf, in_w_bf, wdt_bf, cw, cw, cw, cb, cb, cb, dtb, alog, dsk, nw)


def _proj_ln_kernel(u_ref, w_ref, ssq_ref, x_ref, g_ref, b_ref, o_ref):
    acc = jnp.dot(u_ref[...], w_ref[...], preferred_element_type=F32)
    r = lax.rsqrt(ssq_ref[...] * (1.0 / D_INNER) + RMS_EPS)
    y = ALPHA * x_ref[...] + acc * r
    o_ref[...] = _layer_norm(y, g_ref[...], b_ref[...])


def _proj_ln(u, w_bf, ssq, x, g, b, *, tm=512):
    n_tok = x.shape[0]
    return pl.pallas_call(
        _proj_ln_kernel,
        grid=(n_tok // tm,),
        in_specs=[
            pl.BlockSpec((tm, D_INNER), lambda i: (i, 0)),
            pl.BlockSpec((D_INNER, D_MODEL), lambda i: (0, 0), pipeline_mode=pl.Buffered(1)),
            pl.BlockSpec((tm, 1), lambda i: (i, 0)),
            pl.BlockSpec((tm, D_MODEL), lambda i: (i, 0)),
            pl.BlockSpec((1, D_MODEL), lambda i: (0, 0)),
            pl.BlockSpec((1, D_MODEL), lambda i: (0, 0)),
        ],
        out_specs=pl.BlockSpec((tm, D_MODEL), lambda i: (i, 0)),
        out_shape=jax.ShapeDtypeStruct((n_tok, D_MODEL), F32),
        compiler_params=pltpu.CompilerParams(
            dimension_semantics=("parallel",),
            vmem_limit_bytes=VMEM_LIMIT),
        name="proj_ln",
    )(u, w_bf, ssq, x, g, b)


def _pad_heads(a):
    pad = [(0, 0)] * (a.ndim - 1) + [(0, LANES - N_HEADS)]
    return jnp.pad(a, pad)


def kernel(x, p, pool_w, pool_scale, ssm_in_w, ssm_conv_w, ssm_conv_b, ssm_dt_bias,
           ssm_a_log, ssm_d, ssm_norm_w, ssm_out_w, mlp_w1, mlp_w2, ln_g, ln_b,
           ple_w, ple_gate_w):
    n_tok = BATCH * SEQ
    row = lambda a: a.reshape(1, -1)

    x1 = _pool_ln(x, pool_w[0].astype(BF16), row(pool_scale[0]),
                  row(ln_g[0, 0]), row(ln_b[0, 0]))
    w1_bf, w2_bf = mlp_w1.astype(BF16), mlp_w2.astype(BF16)
    plew_bf, gatew_bf = ple_w.astype(BF16), ple_gate_w.astype(BF16)
    p_tok = p.reshape(DEPTH, n_tok, PLE_DIM)
    x2, x2b = _mlp_ln_ple(0, x1.reshape(n_tok, D_MODEL), w1_bf, w2_bf,
                          row(ln_g[0, 1]), row(ln_b[0, 1]), p_tok, plew_bf, gatew_bf)

    in_w_bf = ssm_in_w[0].astype(BF16)
    wdt = _pad_heads(in_w_bf[:, 2 * D_INNER + 2 * N_GROUPS * D_STATE:])
    yg, ssq = _ssd_scan(
        x2b.reshape(BATCH, SEQ, D_MODEL), in_w_bf, wdt, ssm_conv_w[0], row(ssm_conv_b[0]),
        row(_pad_heads(ssm_dt_bias[0])), row(_pad_heads(ssm_a_log[0])),
        row(jnp.repeat(ssm_d[0], HEAD_DIM)), row(ssm_norm_w[0]))
    x3 = _proj_ln(yg.reshape(n_tok, D_INNER), ssm_out_w[0].astype(BF16),
                  ssq.reshape(n_tok, 1), x2, row(ln_g[1, 0]), row(ln_b[1, 0]))
    x4, _ = _mlp_ln_ple(1, x3, w1_bf, w2_bf,
                        row(ln_g[1, 1]), row(ln_b[1, 1]), p_tok, plew_bf, gatew_bf)
    return x4.reshape(BATCH, SEQ, D_MODEL)
```

```python
import functools

import jax
import jax.numpy as jnp
from jax import lax
from jax.experimental import pallas as pl
from jax.experimental.pallas import tpu as pltpu

F32 = jnp.float32
BF16 = jnp.bfloat16

D_MODEL = 2048
BATCH = 8
SEQ = 2048
DEPTH = 2
ALPHA = (2.0 * DEPTH) ** 0.25
LN_EPS = 1e-5

POOL_WINDOWS = (2, 4, 8, 16)
POOL_GROUP_DIM = D_MODEL // len(POOL_WINDOWS)
POOL_HALO = 16

D_INNER = 2 * D_MODEL
HEAD_DIM = 64
N_HEADS = D_INNER // HEAD_DIM
N_GROUPS = 8
HEADS_PER_GROUP = N_HEADS // N_GROUPS
D_STATE = 128
CONV_WIDTH = 4
CHUNK = 128
GROUP_INNER = D_INNER // N_GROUPS
GROUP_CONV = GROUP_INNER + 2 * D_STATE
RMS_EPS = 1e-5
N_ITEMS = BATCH * N_GROUPS

D_FF = 4 * D_MODEL
PLE_DIM = 256
EPILOGUE_ROWS = 256

LANES = 128
SUBLANES = 8
CONV_PAD = SUBLANES
CONV_BLOCKS = GROUP_CONV // LANES
PROJ_COLS = 256
CHUNKS_PER_ITER = 2
PROJ_ROWS = CHUNKS_PER_ITER * CHUNK
PIECE_AT_FILL = (2, 4, 6, 8, 10)
VMEM_LIMIT = 60 * 1024 * 1024


def _silu(x):
    h = 0.5 * x
    return h + h * jnp.tanh(h)


def _layer_norm(y, g, b):
    mu = jnp.mean(y, axis=-1, keepdims=True)
    d = y - mu
    var = jnp.mean(d * d, axis=-1, keepdims=True)
    return d * lax.rsqrt(var + LN_EPS) * g + b


def _pool_ln_kernel(x_ref, halo_ref, w_ref, scale_ref, g_ref, b_ref, o_ref, *, ts):
    i = pl.program_id(1)
    halo_on = (i > 0).astype(F32)
    pos = lax.broadcasted_iota(jnp.int32, (ts, 1), 0) + i * ts
    for gi, win in enumerate(POOL_WINDOWS):
        sl = slice(gi * POOL_GROUP_DIM, (gi + 1) * POOL_GROUP_DIM)
        xg = x_ref[0, :, sl]
        s = jnp.concatenate([halo_ref[0, :, sl] * halo_on, xg], axis=0)
        k = 1
        while k < win:
            s = s + pltpu.roll(s, k, axis=0)
            k *= 2
        cnt = jnp.minimum(pos + 1, win).astype(F32)
        pooled = s[POOL_HALO:] / cnt - xg
        h = jnp.dot(pooled.astype(BF16), w_ref[gi], preferred_element_type=F32)
        o_ref[0, :, sl] = ALPHA * xg + h * scale_ref[:, sl]
    o_ref[0] = _layer_norm(o_ref[0], g_ref[...], b_ref[...])


def _pool_ln(x, w_bf, scale, g, b, *, ts=256):
    nt = SEQ // ts
    hb = ts // POOL_HALO
    return pl.pallas_call(
        functools.partial(_pool_ln_kernel, ts=ts),
        grid=(BATCH, nt),
        in_specs=[
            pl.BlockSpec((1, ts, D_MODEL), lambda b_, i: (b_, i, 0)),
            pl.BlockSpec((1, POOL_HALO, D_MODEL),
                         lambda b_, i: (b_, jnp.maximum(i * hb - 1, 0), 0)),
            pl.BlockSpec((len(POOL_WINDOWS), POOL_GROUP_DIM, POOL_GROUP_DIM),
                         lambda b_, i: (0, 0, 0)),
            pl.BlockSpec((1, D_MODEL), lambda b_, i: (0, 0)),
            pl.BlockSpec((1, D_MODEL), lambda b_, i: (0, 0)),
            pl.BlockSpec((1, D_MODEL), lambda b_, i: (0, 0)),
        ],
        out_specs=pl.BlockSpec((1, ts, D_MODEL), lambda b_, i: (b_, i, 0)),
        out_shape=jax.ShapeDtypeStruct((BATCH, SEQ, D_MODEL), F32),
        compiler_params=pltpu.CompilerParams(
            dimension_semantics=("parallel", "arbitrary"),
            vmem_limit_bytes=VMEM_LIMIT),
        name="pool_ln",
    )(x, x, w_bf, scale, g, b)


def _mlp_kernel(x_ref, w1_ref, w2_ref, g_ref, b_ref, p_ref, plew_ref, gatew_ref,
                o_ref, *rest, n_chunks):
    ob_ref, xb_s = rest if len(rest) == 2 else (None, rest[0])
    j = pl.program_id(1)

    @pl.when(j == 0)
    def _():
        x = x_ref[...]
        xb_s[...] = x.astype(BF16)
        o_ref[...] = ALPHA * x

    h = jnp.dot(xb_s[...], w1_ref[...], preferred_element_type=F32)
    h = jnp.maximum(h, 0.0)
    h = h * h
    o_ref[...] += jnp.dot(h.astype(BF16), w2_ref[...], preferred_element_type=F32)

    @pl.when(j == n_chunks - 1)
    def _():
        def row_block(r, carry):
            rs = pl.ds(pl.multiple_of(r * EPILOGUE_ROWS, EPILOGUE_ROWS), EPILOGUE_ROWS)
            x2 = _layer_norm(o_ref[rs, :], g_ref[...], b_ref[...])
            gate = jax.nn.sigmoid(
                jnp.dot(x2.astype(BF16), gatew_ref[...], preferred_element_type=F32))
            pe = jnp.dot(p_ref[rs, :].astype(BF16), plew_ref[...],
                         preferred_element_type=F32)
            out = x2 + gate * pe
            o_ref[rs, :] = out
            if ob_ref is not None:
                ob_ref[rs, :] = out.astype(BF16)
            return carry
        lax.fori_loop(0, o_ref.shape[0] // EPILOGUE_ROWS, row_block, 0)


def _mlp_ln_ple(layer, x, w1_bf, w2_bf, g, b, p, plew_bf, gatew_bf, *, tm, tf, emit_bf16):
    n_tok = x.shape[0]
    n_chunks = D_FF // tf
    out_specs = [pl.BlockSpec((tm, D_MODEL), lambda i, j: (i, 0))]
    out_shape = [jax.ShapeDtypeStruct((n_tok, D_MODEL), F32)]
    if emit_bf16:
        out_specs.append(pl.BlockSpec((tm, D_MODEL), lambda i, j: (i, 0)))
        out_shape.append(jax.ShapeDtypeStruct((n_tok, D_MODEL), BF16))
    x_mode = dict(pipeline_mode=pl.Buffered(1)) if tm > 512 else {}
    outs = pl.pallas_call(
        functools.partial(_mlp_kernel, n_chunks=n_chunks),
        grid=(n_tok // tm, n_chunks),
        in_specs=[
            pl.BlockSpec((tm, D_MODEL), lambda i, j: (i, 0), **x_mode),
            pl.BlockSpec((None, D_MODEL, tf), lambda i, j: (layer, 0, j)),
            pl.BlockSpec((None, tf, D_MODEL), lambda i, j: (layer, j, 0)),
            pl.BlockSpec((1, D_MODEL), lambda i, j: (0, 0)),
            pl.BlockSpec((1, D_MODEL), lambda i, j: (0, 0)),
            pl.BlockSpec((None, tm, PLE_DIM), lambda i, j: (layer, i, 0)),
            pl.BlockSpec((None, PLE_DIM, D_MODEL), lambda i, j: (layer, 0, 0),
                         pipeline_mode=pl.Buffered(1)),
            pl.BlockSpec((None, D_MODEL, D_MODEL), lambda i, j: (layer, 0, 0),
                         pipeline_mode=pl.Buffered(1)),
        ],
        out_specs=out_specs,
        out_shape=out_shape,
        scratch_shapes=[pltpu.VMEM((tm, D_MODEL), BF16)],
        compiler_params=pltpu.CompilerParams(
            dimension_semantics=("parallel", "arbitrary"),
            vmem_limit_bytes=VMEM_LIMIT),
        name="mlp_ln_ple",
    )(x, w1_bf, w2_bf, g, b, p, plew_bf, gatew_bf)
    return (outs[0], outs[1]) if emit_bf16 else (outs[0], None)


def _ssd_kernel(xb_ref, wz_ref, wx_ref, wb_ref, wc_ref, wdt_ref, cwx_ref, cwb_ref, cwc_ref,
                cbx_ref, cbb_ref, cbc_ref, dtb_ref, alog_ref, dsk_ref, nw_ref,
                yg_ref, ssq_ref, z0_s, z1_s, xbc0_s, xbc1_s, wbc_s, dtt_s, acst_s, state_s):
    s = pl.program_id(0)
    t = jnp.minimum(s, N_ITEMS - 1)
    t_bslot = (t // N_GROUPS) & 1
    sc = jnp.maximum(s - 1, 0)
    g = sc % N_GROUPS
    s_bslot = (sc // N_GROUPS) & 1
    slots = ((z0_s, xbc0_s), (z1_s, xbc1_s))

    @pl.when((t % N_GROUPS == 0) & (s < N_ITEMS))
    def _():
        dt_raw = jnp.dot(xb_ref[0], wdt_ref[...], preferred_element_type=F32) + dtb_ref[...]
        dt = jnp.maximum(dt_raw, 0.0) + jnp.log1p(jnp.exp(-jnp.abs(dt_raw)))
        a = dt * (-jnp.exp(alog_ref[...]))
        in_chunk = lax.broadcasted_iota(jnp.int32, (SEQ, 1), 0) & (CHUNK - 1)
        k = 1
        while k < CHUNK:
            a = a + jnp.where(in_chunk >= k, pltpu.roll(a, k, axis=0), 0.0)
            k *= 2
        for c in range(SEQ // CHUNK):
            dtt_s[t_bslot, c] = dt[c * CHUNK:(c + 1) * CHUNK, :].T
            acst_s[t_bslot, c] = a[c * CHUNK:(c + 1) * CHUNK, :].T

    @pl.when(s < N_ITEMS)
    def _():
        wbc_s[:, :D_STATE] = wb_ref[...]
        wbc_s[:, D_STATE:] = wc_ref[...]

    @pl.when(s == 0)
    def _():
        for _, xbc_s in slots:
            for j in range(CONV_BLOCKS):
                xbc_s[j, 0:CONV_PAD, :] = jnp.zeros((CONV_PAD, LANES), F32)

    def proj_pieces(c, z_s, xbc_s):
        r0 = pl.multiple_of(c * PROJ_ROWS, PROJ_ROWS)
        xr = xb_ref[0, pl.ds(r0, PROJ_ROWS), :]

        def z_piece(n):
            cols = slice(n * PROJ_COLS, (n + 1) * PROJ_COLS)
            z_s[pl.ds(r0, PROJ_ROWS), cols] = jnp.dot(
                xr, wz_ref[:, cols], preferred_element_type=F32)

        def xbc_piece(n):
            if n < GROUP_INNER // PROJ_COLS:
                w = wx_ref[:, n * PROJ_COLS:(n + 1) * PROJ_COLS]
            else:
                w = wbc_s[...]
            res = jnp.dot(xr, w, preferred_element_type=F32)
            for jj in range(PROJ_COLS // LANES):
                j = n * (PROJ_COLS // LANES) + jj
                xbc_s[j, pl.ds(r0 + CONV_PAD, PROJ_ROWS), :] = res[:, jj * LANES:(jj + 1) * LANES]

        return ([functools.partial(z_piece, n) for n in range(GROUP_INNER // PROJ_COLS)]
                + [functools.partial(xbc_piece, n) for n in range(GROUP_CONV // PROJ_COLS)])

    lo = lax.broadcasted_iota(jnp.int32, (1, LANES), 1) < HEAD_DIM
    ri = lax.broadcasted_iota(jnp.int32, (CHUNK, CHUNK), 0)
    ci = lax.broadcasted_iota(jnp.int32, (CHUNK, CHUNK), 1)
    causal = ri >= ci


    def conv_chunk(c, xbc_s, fill):
        cw = jnp.concatenate([cwx_ref[...], cwb_ref[...], cwc_ref[...]], axis=1)
        cb = jnp.concatenate([cbx_ref[...], cbb_ref[...], cbc_ref[...]], axis=1)
        r0 = pl.multiple_of(c * CHUNK, CHUNK)
        u = []
        for j in range(CONV_BLOCKS):
            bl = slice(j * LANES, (j + 1) * LANES)
            acc = cb[:, bl]
            for k in range(CONV_WIDTH):
                off = CONV_PAD - (CONV_WIDTH - 1) + k
                acc = acc + xbc_s[j, pl.ds(r0 + off, CHUNK), :] * cw[k:k + 1, bl]
            u.append(_silu(acc))
            if j % 2 == 1:
                fill()
        return u

    def head_pair_columns(rows):
        stacked = jnp.concatenate(
            [jnp.broadcast_to(r, (HEAD_DIM, CHUNK)) for r in rows], axis=0)
        return stacked.T

    def scan_chunk(c, u, z_s, fill):
        dsk = dsk_ref[...]
        nw = nw_ref[...]
        r0 = pl.multiple_of(c * CHUNK, CHUNK)
        rows = pl.ds(r0, CHUNK)
        bm = u[CONV_BLOCKS - 2]
        cm = u[CONV_BLOCKS - 1]
        bm_bf = bm.astype(BF16)
        cm_bf = cm.astype(BF16)
        cbm = lax.dot_general(cm_bf, bm_bf, (((1,), (1,)), ((), ())),
                              preferred_element_type=F32)
        state = state_s[...]
        y_off = jnp.dot(cm_bf, state.astype(BF16), preferred_element_type=F32)

        prep = []
        for kb in range(HEADS_PER_GROUP // 2):
            heads = [g * HEADS_PER_GROUP + 2 * kb + hh for hh in range(2)]
            a_rows = [acst_s[s_bslot, c, pl.ds(h, 1), :] for h in heads]
            d_rows = [dtt_s[s_bslot, c, pl.ds(h, 1), :] for h in heads]
            a_blk = head_pair_columns(a_rows)
            dt_blk = head_pair_columns(d_rows)
            a_end = a_blk[CHUNK - 1:CHUNK, :]
            xdt = u[kb] * dt_blk
            lmats = []
            for a_row in a_rows:
                a_col = jnp.broadcast_to(a_row, (CHUNK, CHUNK)).T
                lmats.append(jnp.exp(jnp.where(causal, a_col - a_row, -jnp.inf)))
            rhs = jnp.concatenate([jnp.where(lo, xdt, 0.0), jnp.where(lo, 0.0, xdt)],
                                  axis=0).astype(BF16)
            prep.append((lmats, rhs, jnp.exp(a_blk), xdt * jnp.exp(a_end - a_blk),
                         jnp.exp(a_end)))
            if kb % 2 == 1:
                fill()
        xdec = jnp.concatenate([p[3] for p in prep], axis=1).astype(BF16)
        cdec = jnp.concatenate([p[4] for p in prep], axis=1)
        st_new = jnp.dot(bm.T.astype(BF16), xdec, preferred_element_type=F32)
        state_s[...] = state * cdec + st_new

        y_blocks = []
        for kb, (lmats, rhs, decay, _, _) in enumerate(prep):
            bl = slice(kb * LANES, (kb + 1) * LANES)
            lhs = jnp.concatenate([(cbm * lm).astype(BF16) for lm in lmats], axis=1)
            y_diag = jnp.dot(lhs, rhs, preferred_element_type=F32)
            y_blocks.append(y_diag + y_off[:, bl] * decay + u[kb] * dsk[:, bl])
        y = jnp.concatenate(y_blocks, axis=1)

        zc = z_s[rows, :]
        v = y * _silu(zc)
        ssq_ref[0, rows, :] += jnp.sum(v * v, axis=1, keepdims=True)
        yg_ref[0, rows, :] = (v * nw).astype(BF16)

    def run(scan_slot, proj_slot):
        def body(c, carry):
            pieces = iter(proj_pieces(c, *slots[proj_slot]) if proj_slot is not None else ())

            calls = [0]

            def fill():
                calls[0] += 1
                if calls[0] in PIECE_AT_FILL:
                    piece = next(pieces, None)
                    if piece is not None:
                        piece()

            if scan_slot is not None:
                z_s, xbc_s = slots[scan_slot]
                chunks = [c * CHUNKS_PER_ITER + cc for cc in range(CHUNKS_PER_ITER)]
                us = [conv_chunk(ch, xbc_s, fill) for ch in chunks]
                for ch, u in zip(chunks, us):
                    scan_chunk(ch, u, z_s, fill)
            for piece in pieces:
                piece()
            return carry
        lax.fori_loop(0, SEQ // PROJ_ROWS, body, 0)

    @pl.when(s == 0)
    def _():
        run(None, 0)

    @pl.when(s > 0)
    def _():
        state_s[...] = jnp.zeros_like(state_s)

        @pl.when(g == 0)
        def _():
            ssq_ref[...] = jnp.zeros_like(ssq_ref)

    for parity in range(2):
        @pl.when((s > 0) & (s < N_ITEMS) & (s % 2 == parity))
        def _():
            run(1 - parity, parity)

    @pl.when(s == N_ITEMS)
    def _():
        run((N_ITEMS - 1) % 2, None)


def _ssd_scan(xb, in_w_bf, wdt_bf, cw, cb, dtb, alog, dsk, nw):
    x_blk0 = D_INNER // GROUP_INNER
    b_blk0 = 2 * D_INNER // D_STATE
    c_blk0 = b_blk0 + N_GROUPS
    cb_blk0 = D_INNER // D_STATE
    cc_blk0 = cb_blk0 + N_GROUPS
    proj_b = lambda s: jnp.minimum(s, N_ITEMS - 1) // N_GROUPS
    proj_g = lambda s: jnp.minimum(s, N_ITEMS - 1) % N_GROUPS
    scan_b = lambda s: jnp.maximum(s - 1, 0) // N_GROUPS
    scan_g = lambda s: jnp.maximum(s - 1, 0) % N_GROUPS
    return pl.pallas_call(
        _ssd_kernel,
        grid=(N_ITEMS + 1,),
        in_specs=[
            pl.BlockSpec((1, SEQ, D_MODEL), lambda s: (proj_b(s), 0, 0),
                         pipeline_mode=pl.Buffered(1)),
            pl.BlockSpec((D_MODEL, GROUP_INNER), lambda s: (0, proj_g(s))),
            pl.BlockSpec((D_MODEL, GROUP_INNER), lambda s: (0, x_blk0 + proj_g(s))),
            pl.BlockSpec((D_MODEL, D_STATE), lambda s: (0, b_blk0 + proj_g(s))),
            pl.BlockSpec((D_MODEL, D_STATE), lambda s: (0, c_blk0 + proj_g(s))),
            pl.BlockSpec((D_MODEL, LANES), lambda s: (0, 0)),
            pl.BlockSpec((CONV_WIDTH, GROUP_INNER), lambda s: (0, scan_g(s))),
            pl.BlockSpec((CONV_WIDTH, D_STATE), lambda s: (0, cb_blk0 + scan_g(s))),
            pl.BlockSpec((CONV_WIDTH, D_STATE), lambda s: (0, cc_blk0 + scan_g(s))),
            pl.BlockSpec((1, GROUP_INNER), lambda s: (0, scan_g(s))),
            pl.BlockSpec((1, D_STATE), lambda s: (0, cb_blk0 + scan_g(s))),
            pl.BlockSpec((1, D_STATE), lambda s: (0, cc_blk0 + scan_g(s))),
            pl.BlockSpec((1, LANES), lambda s: (0, 0)),
            pl.BlockSpec((1, LANES), lambda s: (0, 0)),
            pl.BlockSpec((1, GROUP_INNER), lambda s: (0, scan_g(s))),
            pl.BlockSpec((1, GROUP_INNER), lambda s: (0, scan_g(s))),
        ],
        out_specs=[
            pl.BlockSpec((1, SEQ, GROUP_INNER), lambda s: (scan_b(s), 0, scan_g(s))),
            pl.BlockSpec((1, SEQ, 1), lambda s: (scan_b(s), 0, 0)),
        ],
        out_shape=[
            jax.ShapeDtypeStruct((BATCH, SEQ, D_INNER), BF16),
            jax.ShapeDtypeStruct((BATCH, SEQ, 1), F32),
        ],
        scratch_shapes=[
            pltpu.VMEM((SEQ, GROUP_INNER), F32),
            pltpu.VMEM((SEQ, GROUP_INNER), F32),
            pltpu.VMEM((CONV_BLOCKS, SEQ + CONV_PAD, LANES), F32),
            pltpu.VMEM((CONV_BLOCKS, SEQ + CONV_PAD, LANES), F32),
            pltpu.VMEM((D_MODEL, 2 * D_STATE), BF16),
            pltpu.VMEM((2, SEQ // CHUNK, LANES, CHUNK), F32),
            pltpu.VMEM((2, SEQ // CHUNK, LANES, CHUNK), F32),
            pltpu.VMEM((D_STATE, GROUP_INNER), F32),
        ],
        compiler_params=pltpu.CompilerParams(
            dimension_semantics=("arbitrary",),
            vmem_limit_bytes=VMEM_LIMIT),
        name="ssd_scan",
    )(xb, in_w_bf, in_w_bf, in_w_bf, in_w_bf, wdt_bf, cw, cw, cw, cb, cb, cb, dtb, alog, dsk, nw)


def _proj_ln_kernel(u_ref, w_ref, ssq_ref, x_ref, g_ref, b_ref, o_ref):
    acc = jnp.dot(u_ref[...], w_ref[...], preferred_element_type=F32)
    r = lax.rsqrt(ssq_ref[...] * (1.0 / D_INNER) + RMS_EPS)
    y = ALPHA * x_ref[...] + acc * r
    o_ref[...] = _layer_norm(y, g_ref[...], b_ref[...])


def _proj_ln(u, w_bf, ssq, x, g, b, *, tm=512):
    n_tok = x.shape[0]
    return pl.pallas_call(
        _proj_ln_kernel,
        grid=(n_tok // tm,),
        in_specs=[
            pl.BlockSpec((tm, D_INNER), lambda i: (i, 0)),
            pl.BlockSpec((D_INNER, D_MODEL), lambda i: (0, 0), pipeline_mode=pl.Buffered(1)),
            pl.BlockSpec((tm, 1), lambda i: (i, 0)),
            pl.BlockSpec((tm, D_MODEL), lambda i: (i, 0)),
            pl.BlockSpec((1, D_MODEL), lambda i: (0, 0)),
            pl.BlockSpec((1, D_MODEL), lambda i: (0, 0)),
        ],
        out_specs=pl.BlockSpec((tm, D_MODEL), lambda i: (i, 0)),
        out_shape=jax.ShapeDtypeStruct((n_tok, D_MODEL), F32),
        compiler_params=pltpu.CompilerParams(
            dimension_semantics=("parallel",),
            vmem_limit_bytes=VMEM_LIMIT),
        name="proj_ln",
    )(u, w_bf, ssq, x, g, b)


def _pad_heads(a):
    pad = [(0, 0)] * (a.ndim - 1) + [(0, LANES - N_HEADS)]
    return jnp.pad(a, pad)


def kernel(x, p, pool_w, pool_scale, ssm_in_w, ssm_conv_w, ssm_conv_b, ssm_dt_bias,
           ssm_a_log, ssm_d, ssm_norm_w, ssm_out_w, mlp_w1, mlp_w2, ln_g, ln_b,
           ple_w, ple_gate_w):
    n_tok = BATCH * SEQ
    row = lambda a: a.reshape(1, -1)

    x1 = _pool_ln(x, pool_w[0].astype(BF16), row(pool_scale[0]),
                  row(ln_g[0, 0]), row(ln_b[0, 0]))
    w1_bf, w2_bf = mlp_w1.astype(BF16), mlp_w2.astype(BF16)
    plew_bf, gatew_bf = ple_w.astype(BF16), ple_gate_w.astype(BF16)
    p_tok = p.reshape(DEPTH, n_tok, PLE_DIM)
    x2, x2b = _mlp_ln_ple(0, x1.reshape(n_tok, D_MODEL), w1_bf, w2_bf,
                          row(ln_g[0, 1]), row(ln_b[0, 1]), p_tok, plew_bf, gatew_bf,
                          tm=512, tf=1024, emit_bf16=True)

    in_w_bf = ssm_in_w[0].astype(BF16)
    wdt = _pad_heads(in_w_bf[:, 2 * D_INNER + 2 * N_GROUPS * D_STATE:])
    yg, ssq = _ssd_scan(
        x2b.reshape(BATCH, SEQ, D_MODEL), in_w_bf, wdt, ssm_conv_w[0], row(ssm_conv_b[0]),
        row(_pad_heads(ssm_dt_bias[0])), row(_pad_heads(ssm_a_log[0])),
        row(jnp.repeat(ssm_d[0], HEAD_DIM)), row(ssm_norm_w[0]))
    x3 = _proj_ln(yg.reshape(n_tok, D_INNER), ssm_out_w[0].astype(BF16),
                  ssq.reshape(n_tok, 1), x2, row(ln_g[1, 0]), row(ln_b[1, 0]))
    x4, _ = _mlp_ln_ple(1, x3, w1_bf, w2_bf,
                        row(ln_g[1, 1]), row(ln_b[1, 1]), p_tok, plew_bf, gatew_bf,
                        tm=1024, tf=512, emit_bf16=False)
    return x4.reshape(BATCH, SEQ, D_MODEL)
```

```python
import functools

import jax
import jax.numpy as jnp
from jax import lax
from jax.experimental import pallas as pl
from jax.experimental.pallas import tpu as pltpu

F32 = jnp.float32
BF16 = jnp.bfloat16

D_MODEL = 2048
BATCH = 8
SEQ = 2048
DEPTH = 2
ALPHA = (2.0 * DEPTH) ** 0.25
LN_EPS = 1e-5

POOL_WINDOWS = (2, 4, 8, 16)
POOL_GROUP_DIM = D_MODEL // len(POOL_WINDOWS)
POOL_HALO = 16

D_INNER = 2 * D_MODEL
HEAD_DIM = 64
N_HEADS = D_INNER // HEAD_DIM
N_GROUPS = 8
HEADS_PER_GROUP = N_HEADS // N_GROUPS
D_STATE = 128
CONV_WIDTH = 4
CHUNK = 128
GROUP_INNER = D_INNER // N_GROUPS
GROUP_CONV = GROUP_INNER + 2 * D_STATE
RMS_EPS = 1e-5
N_ITEMS = BATCH * N_GROUPS

D_FF = 4 * D_MODEL
PLE_DIM = 256

LANES = 128
SUBLANES = 8
CONV_PAD = SUBLANES
CONV_BLOCKS = GROUP_CONV // LANES
PROJ_COLS = 256
CHUNKS_PER_ITER = 2
PROJ_ROWS = CHUNKS_PER_ITER * CHUNK
PIECE_AT_FILL = (2, 4, 6, 8, 10)
VMEM_LIMIT = 60 * 1024 * 1024


def _silu(x):
    h = 0.5 * x
    return h + h * jnp.tanh(h)


def _layer_norm(y, g, b):
    mu = jnp.mean(y, axis=-1, keepdims=True)
    d = y - mu
    var = jnp.mean(d * d, axis=-1, keepdims=True)
    return d * lax.rsqrt(var + LN_EPS) * g + b


def _pool_ln_kernel(x_ref, halo_ref, w_ref, scale_ref, g_ref, b_ref, o_ref, *, ts):
    i = pl.program_id(1)
    halo_on = (i > 0).astype(F32)
    pos = lax.broadcasted_iota(jnp.int32, (ts, 1), 0) + i * ts
    for gi, win in enumerate(POOL_WINDOWS):
        sl = slice(gi * POOL_GROUP_DIM, (gi + 1) * POOL_GROUP_DIM)
        xg = x_ref[0, :, sl]
        s = jnp.concatenate([halo_ref[0, :, sl] * halo_on, xg], axis=0)
        k = 1
        while k < win:
            s = s + pltpu.roll(s, k, axis=0)
            k *= 2
        cnt = jnp.minimum(pos + 1, win).astype(F32)
        pooled = s[POOL_HALO:] / cnt - xg
        h = jnp.dot(pooled.astype(BF16), w_ref[gi], preferred_element_type=F32)
        o_ref[0, :, sl] = ALPHA * xg + h * scale_ref[:, sl]
    o_ref[0] = _layer_norm(o_ref[0], g_ref[...], b_ref[...])


def _pool_ln(x, w_bf, scale, g, b, *, ts=256):
    nt = SEQ // ts
    hb = ts // POOL_HALO
    return pl.pallas_call(
        functools.partial(_pool_ln_kernel, ts=ts),
        grid=(BATCH, nt),
        in_specs=[
            pl.BlockSpec((1, ts, D_MODEL), lambda b_, i: (b_, i, 0)),
            pl.BlockSpec((1, POOL_HALO, D_MODEL),
                         lambda b_, i: (b_, jnp.maximum(i * hb - 1, 0), 0)),
            pl.BlockSpec((len(POOL_WINDOWS), POOL_GROUP_DIM, POOL_GROUP_DIM),
                         lambda b_, i: (0, 0, 0)),
            pl.BlockSpec((1, D_MODEL), lambda b_, i: (0, 0)),
            pl.BlockSpec((1, D_MODEL), lambda b_, i: (0, 0)),
            pl.BlockSpec((1, D_MODEL), lambda b_, i: (0, 0)),
        ],
        out_specs=pl.BlockSpec((1, ts, D_MODEL), lambda b_, i: (b_, i, 0)),
        out_shape=jax.ShapeDtypeStruct((BATCH, SEQ, D_MODEL), F32),
        compiler_params=pltpu.CompilerParams(
            dimension_semantics=("parallel", "arbitrary"),
            vmem_limit_bytes=VMEM_LIMIT),
        name="pool_ln",
    )(x, x, w_bf, scale, g, b)


def _mlp_kernel(x_ref, w1_ref, w2_ref, g_ref, b_ref, p_ref, plew_ref, gatew_ref,
                o_ref, *rest, n_chunks):
    ob_ref, xb_s = rest if len(rest) == 2 else (None, rest[0])
    j = pl.program_id(1)

    @pl.when(j == 0)
    def _():
        x = x_ref[...]
        xb_s[...] = x.astype(BF16)
        o_ref[...] = ALPHA * x

    h = jnp.dot(xb_s[...], w1_ref[...], preferred_element_type=F32)
    h = jnp.maximum(h, 0.0)
    h = h * h
    o_ref[...] += jnp.dot(h.astype(BF16), w2_ref[...], preferred_element_type=F32)

    @pl.when(j == n_chunks - 1)
    def _():
        x2 = _layer_norm(o_ref[...], g_ref[...], b_ref[...])
        gate = jax.nn.sigmoid(
            jnp.dot(x2.astype(BF16), gatew_ref[...], preferred_element_type=F32))
        pe = jnp.dot(p_ref[...].astype(BF16), plew_ref[...], preferred_element_type=F32)
        out = x2 + gate * pe
        o_ref[...] = out
        if ob_ref is not None:
            ob_ref[...] = out.astype(BF16)


def _mlp_ln_ple(layer, x, w1_bf, w2_bf, g, b, p, plew_bf, gatew_bf, *, emit_bf16,
                tm=512, tf=1024):
    n_tok = x.shape[0]
    n_chunks = D_FF // tf
    out_specs = [pl.BlockSpec((tm, D_MODEL), lambda i, j: (i, 0))]
    out_shape = [jax.ShapeDtypeStruct((n_tok, D_MODEL), F32)]
    if emit_bf16:
        out_specs.append(pl.BlockSpec((tm, D_MODEL), lambda i, j: (i, 0)))
        out_shape.append(jax.ShapeDtypeStruct((n_tok, D_MODEL), BF16))
    outs = pl.pallas_call(
        functools.partial(_mlp_kernel, n_chunks=n_chunks),
        grid=(n_tok // tm, n_chunks),
        in_specs=[
            pl.BlockSpec((tm, D_MODEL), lambda i, j: (i, 0)),
            pl.BlockSpec((None, D_MODEL, tf), lambda i, j: (layer, 0, j)),
            pl.BlockSpec((None, tf, D_MODEL), lambda i, j: (layer, j, 0)),
            pl.BlockSpec((1, D_MODEL), lambda i, j: (0, 0)),
            pl.BlockSpec((1, D_MODEL), lambda i, j: (0, 0)),
            pl.BlockSpec((None, tm, PLE_DIM), lambda i, j: (layer, i, 0)),
            pl.BlockSpec((None, PLE_DIM, D_MODEL), lambda i, j: (layer, 0, 0),
                         pipeline_mode=pl.Buffered(1)),
            pl.BlockSpec((None, D_MODEL, D_MODEL), lambda i, j: (layer, 0, 0),
                         pipeline_mode=pl.Buffered(1)),
        ],
        out_specs=out_specs,
        out_shape=out_shape,
        scratch_shapes=[pltpu.VMEM((tm, D_MODEL), BF16)],
        compiler_params=pltpu.CompilerParams(
            dimension_semantics=("parallel", "arbitrary"),
            vmem_limit_bytes=VMEM_LIMIT),
        name="mlp_ln_ple",
    )(x, w1_bf, w2_bf, g, b, p, plew_bf, gatew_bf)
    return (outs[0], outs[1]) if emit_bf16 else (outs[0], None)


def _ssd_kernel(xb_ref, wz_ref, wx_ref, wb_ref, wc_ref, wdt_ref, cwx_ref, cwb_ref, cwc_ref,
                cbx_ref, cbb_ref, cbc_ref, dtb_ref, alog_ref, dsk_ref, nw_ref,
                yg_ref, ssq_ref, z0_s, z1_s, xbc0_s, xbc1_s, wbc_s, dtt_s, acst_s, state_s):
    s = pl.program_id(0)
    t = jnp.minimum(s, N_ITEMS - 1)
    t_bslot = (t // N_GROUPS) & 1
    sc = jnp.maximum(s - 1, 0)
    g = sc % N_GROUPS
    s_bslot = (sc // N_GROUPS) & 1
    slots = ((z0_s, xbc0_s), (z1_s, xbc1_s))

    @pl.when((t % N_GROUPS == 0) & (s < N_ITEMS))
    def _():
        dt_raw = jnp.dot(xb_ref[0], wdt_ref[...], preferred_element_type=F32) + dtb_ref[...]
        dt = jnp.maximum(dt_raw, 0.0) + jnp.log1p(jnp.exp(-jnp.abs(dt_raw)))
        a = dt * (-jnp.exp(alog_ref[...]))
        in_chunk = lax.broadcasted_iota(jnp.int32, (SEQ, 1), 0) & (CHUNK - 1)
        k = 1
        while k < CHUNK:
            a = a + jnp.where(in_chunk >= k, pltpu.roll(a, k, axis=0), 0.0)
            k *= 2
        for c in range(SEQ // CHUNK):
            dtt_s[t_bslot, c] = dt[c * CHUNK:(c + 1) * CHUNK, :].T
            acst_s[t_bslot, c] = a[c * CHUNK:(c + 1) * CHUNK, :].T

    @pl.when(s < N_ITEMS)
    def _():
        wbc_s[:, :D_STATE] = wb_ref[...]
        wbc_s[:, D_STATE:] = wc_ref[...]

    @pl.when(s == 0)
    def _():
        for _, xbc_s in slots:
            for j in range(CONV_BLOCKS):
                xbc_s[j, 0:CONV_PAD, :] = jnp.zeros((CONV_PAD, LANES), F32)

    def proj_pieces(c, z_s, xbc_s):
        r0 = pl.multiple_of(c * PROJ_ROWS, PROJ_ROWS)
        xr = xb_ref[0, pl.ds(r0, PROJ_ROWS), :]

        def z_piece(n):
            cols = slice(n * PROJ_COLS, (n + 1) * PROJ_COLS)
            z_s[pl.ds(r0, PROJ_ROWS), cols] = jnp.dot(
                xr, wz_ref[:, cols], preferred_element_type=F32)

        def xbc_piece(n):
            if n < GROUP_INNER // PROJ_COLS:
                w = wx_ref[:, n * PROJ_COLS:(n + 1) * PROJ_COLS]
            else:
                w = wbc_s[...]
            res = jnp.dot(xr, w, preferred_element_type=F32)
            for jj in range(PROJ_COLS // LANES):
                j = n * (PROJ_COLS // LANES) + jj
                xbc_s[j, pl.ds(r0 + CONV_PAD, PROJ_ROWS), :] = res[:, jj * LANES:(jj + 1) * LANES]

        return ([functools.partial(z_piece, n) for n in range(GROUP_INNER // PROJ_COLS)]
                + [functools.partial(xbc_piece, n) for n in range(GROUP_CONV // PROJ_COLS)])

    lo = lax.broadcasted_iota(jnp.int32, (1, LANES), 1) < HEAD_DIM
    ri = lax.broadcasted_iota(jnp.int32, (CHUNK, CHUNK), 0)
    ci = lax.broadcasted_iota(jnp.int32, (CHUNK, CHUNK), 1)
    causal = ri >= ci


    def conv_chunk(c, xbc_s, fill):
        cw = jnp.concatenate([cwx_ref[...], cwb_ref[...], cwc_ref[...]], axis=1)
        cb = jnp.concatenate([cbx_ref[...], cbb_ref[...], cbc_ref[...]], axis=1)
        r0 = pl.multiple_of(c * CHUNK, CHUNK)
        u = []
        for j in range(CONV_BLOCKS):
            bl = slice(j * LANES, (j + 1) * LANES)
            acc = cb[:, bl]
            for k in range(CONV_WIDTH):
                off = CONV_PAD - (CONV_WIDTH - 1) + k
                acc = acc + xbc_s[j, pl.ds(r0 + off, CHUNK), :] * cw[k:k + 1, bl]
            u.append(_silu(acc))
            if j % 2 == 1:
                fill()
        return u

    def head_pair_columns(rows):
        stacked = jnp.concatenate(
            [jnp.broadcast_to(r, (HEAD_DIM, CHUNK)) for r in rows], axis=0)
        return stacked.T

    def scan_chunk(c, u, z_s, fill):
        dsk = dsk_ref[...]
        nw = nw_ref[...]
        r0 = pl.multiple_of(c * CHUNK, CHUNK)
        rows = pl.ds(r0, CHUNK)
        bm = u[CONV_BLOCKS - 2]
        cm = u[CONV_BLOCKS - 1]
        bm_bf = bm.astype(BF16)
        cm_bf = cm.astype(BF16)
        cbm = lax.dot_general(cm_bf, bm_bf, (((1,), (1,)), ((), ())),
                              preferred_element_type=F32)
        state = state_s[...]
        y_off = jnp.dot(cm_bf, state.astype(BF16), preferred_element_type=F32)

        prep = []
        for kb in range(HEADS_PER_GROUP // 2):
            heads = [g * HEADS_PER_GROUP + 2 * kb + hh for hh in range(2)]
            a_rows = [acst_s[s_bslot, c, pl.ds(h, 1), :] for h in heads]
            d_rows = [dtt_s[s_bslot, c, pl.ds(h, 1), :] for h in heads]
            a_blk = head_pair_columns(a_rows)
            dt_blk = head_pair_columns(d_rows)
            a_end = a_blk[CHUNK - 1:CHUNK, :]
            xdt = u[kb] * dt_blk
            lmats = []
            for a_row in a_rows:
                a_col = jnp.broadcast_to(a_row, (CHUNK, CHUNK)).T
                lmats.append(jnp.exp(jnp.where(causal, a_col - a_row, -jnp.inf)))
            rhs = jnp.concatenate([jnp.where(lo, xdt, 0.0), jnp.where(lo, 0.0, xdt)],
                                  axis=0).astype(BF16)
            prep.append((lmats, rhs, jnp.exp(a_blk), xdt * jnp.exp(a_end - a_blk),
                         jnp.exp(a_end)))
            if kb % 2 == 1:
                fill()
        xdec = jnp.concatenate([p[3] for p in prep], axis=1).astype(BF16)
        cdec = jnp.concatenate([p[4] for p in prep], axis=1)
        st_new = jnp.dot(bm.T.astype(BF16), xdec, preferred_element_type=F32)
        state_s[...] = state * cdec + st_new

        y_blocks = []
        for kb, (lmats, rhs, decay, _, _) in enumerate(prep):
            bl = slice(kb * LANES, (kb + 1) * LANES)
            lhs = jnp.concatenate([(cbm * lm).astype(BF16) for lm in lmats], axis=1)
            y_diag = jnp.dot(lhs, rhs, preferred_element_type=F32)
            y_blocks.append(y_diag + y_off[:, bl] * decay + u[kb] * dsk[:, bl])
        y = jnp.concatenate(y_blocks, axis=1)

        zc = z_s[rows, :]
        v = y * _silu(zc)
        ssq_ref[0, rows, :] += jnp.sum(v * v, axis=1, keepdims=True)
        yg_ref[0, rows, :] = (v * nw).astype(BF16)

    def run(scan_slot, proj_slot):
        def body(c, carry):
            pieces = iter(proj_pieces(c, *slots[proj_slot]) if proj_slot is not None else ())

            calls = [0]

            def fill():
                calls[0] += 1
                if calls[0] in PIECE_AT_FILL:
                    piece = next(pieces, None)
                    if piece is not None:
                        piece()

            if scan_slot is not None:
                z_s, xbc_s = slots[scan_slot]
                chunks = [c * CHUNKS_PER_ITER + cc for cc in range(CHUNKS_PER_ITER)]
                us = [conv_chunk(ch, xbc_s, fill) for ch in chunks]
                for ch, u in zip(chunks, us):
                    scan_chunk(ch, u, z_s, fill)
            for piece in pieces:
                piece()
            return carry
        lax.fori_loop(0, SEQ // PROJ_ROWS, body, 0)

    @pl.when(s == 0)
    def _():
        run(None, 0)

    @pl.when(s > 0)
    def _():
        state_s[...] = jnp.zeros_like(state_s)

        @pl.when(g == 0)
        def _():
            ssq_ref[...] = jnp.zeros_like(ssq_ref)

    for parity in range(2):
        @pl.when((s > 0) & (s < N_ITEMS) & (s % 2 == parity))
        def _():
            run(1 - parity, parity)

    @pl.when(s == N_ITEMS)
    def _():
        run((N_ITEMS - 1) % 2, None)


def _ssd_scan(xb, in_w_bf, wdt_bf, cw, cb, dtb, alog, dsk, nw):
    x_blk0 = D_INNER // GROUP_INNER
    b_blk0 = 2 * D_INNER // D_STATE
    c_blk0 = b_blk0 + N_GROUPS
    cb_blk0 = D_INNER // D_STATE
    cc_blk0 = cb_blk0 + N_GROUPS
    proj_b = lambda s: jnp.minimum(s, N_ITEMS - 1) // N_GROUPS
    proj_g = lambda s: jnp.minimum(s, N_ITEMS - 1) % N_GROUPS
    scan_b = lambda s: jnp.maximum(s - 1, 0) // N_GROUPS
    scan_g = lambda s: jnp.maximum(s - 1, 0) % N_GROUPS
    return pl.pallas_call(
        _ssd_kernel,
        grid=(N_ITEMS + 1,),
        in_specs=[
            pl.BlockSpec((1, SEQ, D_MODEL), lambda s: (proj_b(s), 0, 0),
                         pipeline_mode=pl.Buffered(1)),
            pl.BlockSpec((D_MODEL, GROUP_INNER), lambda s: (0, proj_g(s))),
            pl.BlockSpec((D_MODEL, GROUP_INNER), lambda s: (0, x_blk0 + proj_g(s))),
            pl.BlockSpec((D_MODEL, D_STATE), lambda s: (0, b_blk0 + proj_g(s))),
            pl.BlockSpec((D_MODEL, D_STATE), lambda s: (0, c_blk0 + proj_g(s))),
            pl.BlockSpec((D_MODEL, LANES), lambda s: (0, 0)),
            pl.BlockSpec((CONV_WIDTH, GROUP_INNER), lambda s: (0, scan_g(s))),
            pl.BlockSpec((CONV_WIDTH, D_STATE), lambda s: (0, cb_blk0 + scan_g(s))),
            pl.BlockSpec((CONV_WIDTH, D_STATE), lambda s: (0, cc_blk0 + scan_g(s))),
            pl.BlockSpec((1, GROUP_INNER), lambda s: (0, scan_g(s))),
            pl.BlockSpec((1, D_STATE), lambda s: (0, cb_blk0 + scan_g(s))),
            pl.BlockSpec((1, D_STATE), lambda s: (0, cc_blk0 + scan_g(s))),
            pl.BlockSpec((1, LANES), lambda s: (0, 0)),
            pl.BlockSpec((1, LANES), lambda s: (0, 0)),
            pl.BlockSpec((1, GROUP_INNER), lambda s: (0, scan_g(s))),
            pl.BlockSpec((1, GROUP_INNER), lambda s: (0, scan_g(s))),
        ],
        out_specs=[
            pl.BlockSpec((1, SEQ, GROUP_INNER), lambda s: (scan_b(s), 0, scan_g(s))),
            pl.BlockSpec((1, SEQ, 1), lambda s: (scan_b(s), 0, 0)),
        ],
        out_shape=[
            jax.ShapeDtypeStruct((BATCH, SEQ, D_INNER), BF16),
            jax.ShapeDtypeStruct((BATCH, SEQ, 1), F32),
        ],
        scratch_shapes=[
            pltpu.VMEM((SEQ, GROUP_INNER), F32),
            pltpu.VMEM((SEQ, GROUP_INNER), F32),
            pltpu.VMEM((CONV_BLOCKS, SEQ + CONV_PAD, LANES), F32),
            pltpu.VMEM((CONV_BLOCKS, SEQ + CONV_PAD, LANES), F32),
            pltpu.VMEM((D_MODEL, 2 * D_STATE), BF16),
            pltpu.VMEM((2, SEQ // CHUNK, LANES, CHUNK), F32),
            pltpu.VMEM((2, SEQ // CHUNK, LANES, CHUNK), F32),
            pltpu.VMEM((D_STATE, GROUP_INNER), F32),
        ],
        compiler_params=pltpu.CompilerParams(
            dimension_semantics=("arbitrary",),
            vmem_limit_bytes=VMEM_LIMIT),
        name="ssd_scan",
    )(xb, in_w_bf, in_w_bf, in_w_bf, in_w_bf, wdt_bf, cw, cw, cw, cb, cb, cb, dtb, alog, dsk, nw)


def _proj_ln_kernel(u_ref, w_ref, ssq_ref, x_ref, g_ref, b_ref, o_ref):
    acc = jnp.dot(u_ref[...], w_ref[...], preferred_element_type=F32)
    r = lax.rsqrt(ssq_ref[...] * (1.0 / D_INNER) + RMS_EPS)
    y = ALPHA * x_ref[...] + acc * r
    o_ref[...] = _layer_norm(y, g_ref[...], b_ref[...])


def _proj_ln(u, w_bf, ssq, x, g, b, *, tm=512):
    n_tok = x.shape[0]
    return pl.pallas_call(
        _proj_ln_kernel,
        grid=(n_tok // tm,),
        in_specs=[
            pl.BlockSpec((tm, D_INNER), lambda i: (i, 0)),
            pl.BlockSpec((D_INNER, D_MODEL), lambda i: (0, 0), pipeline_mode=pl.Buffered(1)),
            pl.BlockSpec((tm, 1), lambda i: (i, 0)),
            pl.BlockSpec((tm, D_MODEL), lambda i: (i, 0)),
            pl.BlockSpec((1, D_MODEL), lambda i: (0, 0)),
            pl.BlockSpec((1, D_MODEL), lambda i: (0, 0)),
        ],
        out_specs=pl.BlockSpec((tm, D_MODEL), lambda i: (i, 0)),
        out_shape=jax.ShapeDtypeStruct((n_tok, D_MODEL), F32),
        compiler_params=pltpu.CompilerParams(
            dimension_semantics=("parallel",),
            vmem_limit_bytes=VMEM_LIMIT),
        name="proj_ln",
    )(u, w_bf, ssq, x, g, b)


def _pad_heads(a):
    pad = [(0, 0)] * (a.ndim - 1) + [(0, LANES - N_HEADS)]
    return jnp.pad(a, pad)


def kernel(x, p, pool_w, pool_scale, ssm_in_w, ssm_conv_w, ssm_conv_b, ssm_dt_bias,
           ssm_a_log, ssm_d, ssm_norm_w, ssm_out_w, mlp_w1, mlp_w2, ln_g, ln_b,
           ple_w, ple_gate_w):
    n_tok = BATCH * SEQ
    row = lambda a: a.reshape(1, -1)

    x1 = _pool_ln(x, pool_w[0].astype(BF16), row(pool_scale[0]),
                  row(ln_g[0, 0]), row(ln_b[0, 0]))
    w1_bf, w2_bf = mlp_w1.astype(BF16), mlp_w2.astype(BF16)
    plew_bf, gatew_bf = ple_w.astype(BF16), ple_gate_w.astype(BF16)
    p_tok = p.reshape(DEPTH, n_tok, PLE_DIM)
    x2, x2b = _mlp_ln_ple(0, x1.reshape(n_tok, D_MODEL), w1_bf, w2_bf,
                          row(ln_g[0, 1]), row(ln_b[0, 1]), p_tok, plew_bf, gatew_bf,
                          emit_bf16=True)

    in_w_bf = ssm_in_w[0].astype(BF16)
    wdt = _pad_heads(in_w_bf[:, 2 * D_INNER + 2 * N_GROUPS * D_STATE:])
    yg, ssq = _ssd_scan(
        x2b.reshape(BATCH, SEQ, D_MODEL), in_w_bf, wdt, ssm_conv_w[0], row(ssm_conv_b[0]),
        row(_pad_heads(ssm_dt_bias[0])), row(_pad_heads(ssm_a_log[0])),
        row(jnp.repeat(ssm_d[0], HEAD_DIM)), row(ssm_norm_w[0]))
    x3 = _proj_ln(yg.reshape(n_tok, D_INNER), ssm_out_w[0].astype(BF16),
                  ssq.reshape(n_tok, 1), x2, row(ln_g[1, 0]), row(ln_b[1, 0]))
    x4, _ = _mlp_ln_ple(1, x3, w1_bf, w2_bf,
                        row(ln_g[1, 1]), row(ln_b[1, 1]), p_tok, plew_bf, gatew_bf,
                        emit_bf16=False)
    return x4.reshape(BATCH, SEQ, D_MODEL)
```

```python
import functools

import jax
import jax.numpy as jnp
from jax import lax
from jax.experimental import pallas as pl
from jax.experimental.pallas import tpu as pltpu

F32 = jnp.float32
BF16 = jnp.bfloat16

D_MODEL = 2048
BATCH = 8
SEQ = 2048
DEPTH = 2
ALPHA = (2.0 * DEPTH) ** 0.25
LN_EPS = 1e-5

POOL_WINDOWS = (2, 4, 8, 16)
POOL_GROUP_DIM = D_MODEL // len(POOL_WINDOWS)
POOL_HALO = 16

D_INNER = 2 * D_MODEL
HEAD_DIM = 64
N_HEADS = D_INNER // HEAD_DIM
N_GROUPS = 8
HEADS_PER_GROUP = N_HEADS // N_GROUPS
D_STATE = 128
CONV_WIDTH = 4
CHUNK = 128
GROUP_INNER = D_INNER // N_GROUPS
GROUP_CONV = GROUP_INNER + 2 * D_STATE
RMS_EPS = 1e-5
N_ITEMS = BATCH * N_GROUPS

D_FF = 4 * D_MODEL
PLE_DIM = 256
PROJ_LN_SPLIT = 2
MLP_EPILOGUE_SPLIT = 2

LANES = 128
SUBLANES = 8
CONV_PAD = SUBLANES
CONV_BLOCKS = GROUP_CONV // LANES
PROJ_COLS = 256
CHUNKS_PER_ITER = 2
PROJ_ROWS = CHUNKS_PER_ITER * CHUNK
PIECE_AT_FILL = (2, 4, 6, 8, 10)
VMEM_LIMIT = 60 * 1024 * 1024


def _silu(x):
    h = 0.5 * x
    return h + h * jnp.tanh(h)


def _layer_norm(y, g, b):
    mu = jnp.mean(y, axis=-1, keepdims=True)
    d = y - mu
    var = jnp.mean(d * d, axis=-1, keepdims=True)
    return d * lax.rsqrt(var + LN_EPS) * g + b


def _pool_ln_kernel(x_ref, halo_ref, w_ref, scale_ref, g_ref, b_ref, o_ref, *, ts):
    i = pl.program_id(1)
    halo_on = (i > 0).astype(F32)
    pos = lax.broadcasted_iota(jnp.int32, (ts, 1), 0) + i * ts
    for gi, win in enumerate(POOL_WINDOWS):
        sl = slice(gi * POOL_GROUP_DIM, (gi + 1) * POOL_GROUP_DIM)
        xg = x_ref[0, :, sl]
        s = jnp.concatenate([halo_ref[0, :, sl] * halo_on, xg], axis=0)
        k = 1
        while k < win:
            s = s + pltpu.roll(s, k, axis=0)
            k *= 2
        cnt = jnp.minimum(pos + 1, win).astype(F32)
        pooled = s[POOL_HALO:] / cnt - xg
        h = jnp.dot(pooled.astype(BF16), w_ref[gi], preferred_element_type=F32)
        o_ref[0, :, sl] = ALPHA * xg + h * scale_ref[:, sl]
    o_ref[0] = _layer_norm(o_ref[0], g_ref[...], b_ref[...])


def _pool_ln(x, w_bf, scale, g, b, *, ts=256):
    nt = SEQ // ts
    hb = ts // POOL_HALO
    return pl.pallas_call(
        functools.partial(_pool_ln_kernel, ts=ts),
        grid=(BATCH, nt),
        in_specs=[
            pl.BlockSpec((1, ts, D_MODEL), lambda b_, i: (b_, i, 0)),
            pl.BlockSpec((1, POOL_HALO, D_MODEL),
                         lambda b_, i: (b_, jnp.maximum(i * hb - 1, 0), 0)),
            pl.BlockSpec((len(POOL_WINDOWS), POOL_GROUP_DIM, POOL_GROUP_DIM),
                         lambda b_, i: (0, 0, 0)),
            pl.BlockSpec((1, D_MODEL), lambda b_, i: (0, 0)),
            pl.BlockSpec((1, D_MODEL), lambda b_, i: (0, 0)),
            pl.BlockSpec((1, D_MODEL), lambda b_, i: (0, 0)),
        ],
        out_specs=pl.BlockSpec((1, ts, D_MODEL), lambda b_, i: (b_, i, 0)),
        out_shape=jax.ShapeDtypeStruct((BATCH, SEQ, D_MODEL), F32),
        compiler_params=pltpu.CompilerParams(
            dimension_semantics=("parallel", "arbitrary"),
            vmem_limit_bytes=VMEM_LIMIT),
        name="pool_ln",
    )(x, x, w_bf, scale, g, b)


def _mlp_kernel(x_ref, w1_ref, w2_ref, g_ref, b_ref, p_ref, plew_ref, gatew_ref,
                o_ref, *rest, n_chunks):
    ob_ref, xb_s = rest if len(rest) == 2 else (None, rest[0])
    j = pl.program_id(1)

    def chunk():
        h = jnp.dot(xb_s[...], w1_ref[...], preferred_element_type=F32)
        h = jnp.maximum(h, 0.0)
        h = h * h
        return jnp.dot(h.astype(BF16), w2_ref[...], preferred_element_type=F32)

    @pl.when(j == 0)
    def _():
        xb_s[...] = x_ref[...].astype(BF16)
        o_ref[...] = chunk()

    @pl.when(j > 0)
    def _():
        o_ref[...] += chunk()

    @pl.when(j == n_chunks - 1)
    def _():
        rows = o_ref.shape[0] // MLP_EPILOGUE_SPLIT
        for r in range(MLP_EPILOGUE_SPLIT):
            rs = slice(r * rows, (r + 1) * rows)
            x2 = _layer_norm(ALPHA * x_ref[rs, :] + o_ref[rs, :], g_ref[...], b_ref[...])
            gate = jax.nn.sigmoid(
                jnp.dot(x2.astype(BF16), gatew_ref[...], preferred_element_type=F32))
            pe = jnp.dot(p_ref[rs, :].astype(BF16), plew_ref[...],
                         preferred_element_type=F32)
            out = x2 + gate * pe
            o_ref[rs, :] = out
            if ob_ref is not None:
                ob_ref[rs, :] = out.astype(BF16)


def _mlp_ln_ple(layer, x, w1_bf, w2_bf, g, b, p, plew_bf, gatew_bf, *, emit_bf16,
                tm=512, tf=1024):
    n_tok = x.shape[0]
    n_chunks = D_FF // tf
    out_specs = [pl.BlockSpec((tm, D_MODEL), lambda i, j: (i, 0))]
    out_shape = [jax.ShapeDtypeStruct((n_tok, D_MODEL), F32)]
    if emit_bf16:
        out_specs.append(pl.BlockSpec((tm, D_MODEL), lambda i, j: (i, 0)))
        out_shape.append(jax.ShapeDtypeStruct((n_tok, D_MODEL), BF16))
    outs = pl.pallas_call(
        functools.partial(_mlp_kernel, n_chunks=n_chunks),
        grid=(n_tok // tm, n_chunks),
        in_specs=[
            pl.BlockSpec((tm, D_MODEL), lambda i, j: (i, 0)),
            pl.BlockSpec((None, D_MODEL, tf), lambda i, j: (layer, 0, j)),
            pl.BlockSpec((None, tf, D_MODEL), lambda i, j: (layer, j, 0)),
            pl.BlockSpec((1, D_MODEL), lambda i, j: (0, 0)),
            pl.BlockSpec((1, D_MODEL), lambda i, j: (0, 0)),
            pl.BlockSpec((None, tm, PLE_DIM), lambda i, j: (layer, i, 0)),
            pl.BlockSpec((None, PLE_DIM, D_MODEL), lambda i, j: (layer, 0, 0),
                         pipeline_mode=pl.Buffered(1)),
            pl.BlockSpec((None, D_MODEL, D_MODEL), lambda i, j: (layer, 0, 0),
                         pipeline_mode=pl.Buffered(1)),
        ],
        out_specs=out_specs,
        out_shape=out_shape,
        scratch_shapes=[pltpu.VMEM((tm, D_MODEL), BF16)],
        compiler_params=pltpu.CompilerParams(
            dimension_semantics=("parallel", "arbitrary"),
            vmem_limit_bytes=VMEM_LIMIT),
        name="mlp_ln_ple",
    )(x, w1_bf, w2_bf, g, b, p, plew_bf, gatew_bf)
    return (outs[0], outs[1]) if emit_bf16 else (outs[0], None)


def _ssd_kernel(xb_ref, wz_ref, wx_ref, wb_ref, wc_ref, wdt_ref, cwx_ref, cwb_ref, cwc_ref,
                cbx_ref, cbb_ref, cbc_ref, dtb_ref, alog_ref, dsk_ref, nw_ref,
                yg_ref, ssq_ref, z0_s, z1_s, xbc0_s, xbc1_s, wbc_s, dtt_s, acst_s, state_s):
    s = pl.program_id(0)
    t = jnp.minimum(s, N_ITEMS - 1)
    t_bslot = (t // N_GROUPS) & 1
    sc = jnp.maximum(s - 1, 0)
    g = sc % N_GROUPS
    s_bslot = (sc // N_GROUPS) & 1
    slots = ((z0_s, xbc0_s), (z1_s, xbc1_s))

    @pl.when((t % N_GROUPS == 0) & (s < N_ITEMS))
    def _():
        dt_raw = jnp.dot(xb_ref[0], wdt_ref[...], preferred_element_type=F32) + dtb_ref[...]
        dt = jnp.maximum(dt_raw, 0.0) + jnp.log1p(jnp.exp(-jnp.abs(dt_raw)))
        a = dt * (-jnp.exp(alog_ref[...]))
        in_chunk = lax.broadcasted_iota(jnp.int32, (SEQ, 1), 0) & (CHUNK - 1)
        k = 1
        while k < CHUNK:
            a = a + jnp.where(in_chunk >= k, pltpu.roll(a, k, axis=0), 0.0)
            k *= 2
        for c in range(SEQ // CHUNK):
            dtt_s[t_bslot, c] = dt[c * CHUNK:(c + 1) * CHUNK, :].T
            acst_s[t_bslot, c] = a[c * CHUNK:(c + 1) * CHUNK, :].T

    @pl.when(s < N_ITEMS)
    def _():
        wbc_s[:, :D_STATE] = wb_ref[...]
        wbc_s[:, D_STATE:] = wc_ref[...]

    @pl.when(s == 0)
    def _():
        for _, xbc_s in slots:
            for j in range(CONV_BLOCKS):
                xbc_s[j, 0:CONV_PAD, :] = jnp.zeros((CONV_PAD, LANES), F32)

    def proj_pieces(c, z_s, xbc_s):
        r0 = pl.multiple_of(c * PROJ_ROWS, PROJ_ROWS)
        xr = xb_ref[0, pl.ds(r0, PROJ_ROWS), :]

        def z_piece(n):
            cols = slice(n * PROJ_COLS, (n + 1) * PROJ_COLS)
            z_s[pl.ds(r0, PROJ_ROWS), cols] = jnp.dot(
                xr, wz_ref[:, cols], preferred_element_type=F32)

        def xbc_piece(n):
            if n < GROUP_INNER // PROJ_COLS:
                w = wx_ref[:, n * PROJ_COLS:(n + 1) * PROJ_COLS]
            else:
                w = wbc_s[...]
            res = jnp.dot(xr, w, preferred_element_type=F32)
            for jj in range(PROJ_COLS // LANES):
                j = n * (PROJ_COLS // LANES) + jj
                xbc_s[j, pl.ds(r0 + CONV_PAD, PROJ_ROWS), :] = res[:, jj * LANES:(jj + 1) * LANES]

        return ([functools.partial(z_piece, n) for n in range(GROUP_INNER // PROJ_COLS)]
                + [functools.partial(xbc_piece, n) for n in range(GROUP_CONV // PROJ_COLS)])

    lo = lax.broadcasted_iota(jnp.int32, (1, LANES), 1) < HEAD_DIM
    ri = lax.broadcasted_iota(jnp.int32, (CHUNK, CHUNK), 0)
    ci = lax.broadcasted_iota(jnp.int32, (CHUNK, CHUNK), 1)
    causal = ri >= ci


    def conv_chunk(c, xbc_s, fill):
        cw = jnp.concatenate([cwx_ref[...], cwb_ref[...], cwc_ref[...]], axis=1)
        cb = jnp.concatenate([cbx_ref[...], cbb_ref[...], cbc_ref[...]], axis=1)
        r0 = pl.multiple_of(c * CHUNK, CHUNK)
        u = []
        for j in range(CONV_BLOCKS):
            bl = slice(j * LANES, (j + 1) * LANES)
            acc = cb[:, bl]
            for k in range(CONV_WIDTH):
                off = CONV_PAD - (CONV_WIDTH - 1) + k
                acc = acc + xbc_s[j, pl.ds(r0 + off, CHUNK), :] * cw[k:k + 1, bl]
            u.append(_silu(acc))
            if j % 2 == 1:
                fill()
        return u

    def head_pair_columns(rows):
        stacked = jnp.concatenate(
            [jnp.broadcast_to(r, (HEAD_DIM, CHUNK)) for r in rows], axis=0)
        return stacked.T

    def scan_chunk(c, u, z_s, fill):
        dsk = dsk_ref[...]
        nw = nw_ref[...]
        r0 = pl.multiple_of(c * CHUNK, CHUNK)
        rows = pl.ds(r0, CHUNK)
        bm = u[CONV_BLOCKS - 2]
        cm = u[CONV_BLOCKS - 1]
        bm_bf = bm.astype(BF16)
        cm_bf = cm.astype(BF16)
        cbm = lax.dot_general(cm_bf, bm_bf, (((1,), (1,)), ((), ())),
                              preferred_element_type=F32)
        state = state_s[...]
        y_off = jnp.dot(cm_bf, state.astype(BF16), preferred_element_type=F32)

        prep = []
        for kb in range(HEADS_PER_GROUP // 2):
            heads = [g * HEADS_PER_GROUP + 2 * kb + hh for hh in range(2)]
            a_rows = [acst_s[s_bslot, c, pl.ds(h, 1), :] for h in heads]
            d_rows = [dtt_s[s_bslot, c, pl.ds(h, 1), :] for h in heads]
            a_blk = head_pair_columns(a_rows)
            dt_blk = head_pair_columns(d_rows)
            a_end = a_blk[CHUNK - 1:CHUNK, :]
            xdt = u[kb] * dt_blk
            lmats = []
            for a_row in a_rows:
                a_col = jnp.broadcast_to(a_row, (CHUNK, CHUNK)).T
                lmats.append(jnp.exp(jnp.where(causal, a_col - a_row, -jnp.inf)))
            rhs = jnp.concatenate([jnp.where(lo, xdt, 0.0), jnp.where(lo, 0.0, xdt)],
                                  axis=0).astype(BF16)
            prep.append((lmats, rhs, jnp.exp(a_blk), xdt * jnp.exp(a_end - a_blk),
                         jnp.exp(a_end)))
            if kb % 2 == 1:
                fill()
        xdec = jnp.concatenate([p[3] for p in prep], axis=1).astype(BF16)
        cdec = jnp.concatenate([p[4] for p in prep], axis=1)
        st_new = jnp.dot(bm.T.astype(BF16), xdec, preferred_element_type=F32)
        state_s[...] = state * cdec + st_new

        y_blocks = []
        for kb, (lmats, rhs, decay, _, _) in enumerate(prep):
            bl = slice(kb * LANES, (kb + 1) * LANES)
            lhs = jnp.concatenate([(cbm * lm).astype(BF16) for lm in lmats], axis=1)
            y_diag = jnp.dot(lhs, rhs, preferred_element_type=F32)
            y_blocks.append(y_diag + y_off[:, bl] * decay + u[kb] * dsk[:, bl])
        y = jnp.concatenate(y_blocks, axis=1)

        zc = z_s[rows, :]
        v = y * _silu(zc)
        ssq_ref[0, rows, :] += jnp.sum(v * v, axis=1, keepdims=True)
        yg_ref[0, rows, :] = (v * nw).astype(BF16)

    def run(scan_slot, proj_slot):
        def body(c, carry):
            pieces = iter(proj_pieces(c, *slots[proj_slot]) if proj_slot is not None else ())

            calls = [0]

            def fill():
                calls[0] += 1
                if calls[0] in PIECE_AT_FILL:
                    piece = next(pieces, None)
                    if piece is not None:
                        piece()

            if scan_slot is not None:
                z_s, xbc_s = slots[scan_slot]
                chunks = [c * CHUNKS_PER_ITER + cc for cc in range(CHUNKS_PER_ITER)]
                us = [conv_chunk(ch, xbc_s, fill) for ch in chunks]
                for ch, u in zip(chunks, us):
                    scan_chunk(ch, u, z_s, fill)
            for piece in pieces:
                piece()
            return carry
        lax.fori_loop(0, SEQ // PROJ_ROWS, body, 0)

    @pl.when(s == 0)
    def _():
        run(None, 0)

    @pl.when(s > 0)
    def _():
        state_s[...] = jnp.zeros_like(state_s)

        @pl.when(g == 0)
        def _():
            ssq_ref[...] = jnp.zeros_like(ssq_ref)

    for parity in range(2):
        @pl.when((s > 0) & (s < N_ITEMS) & (s % 2 == parity))
        def _():
            run(1 - parity, parity)

    @pl.when(s == N_ITEMS)
    def _():
        run((N_ITEMS - 1) % 2, None)


def _ssd_scan(xb, in_w_bf, wdt_bf, cw, cb, dtb, alog, dsk, nw):
    x_blk0 = D_INNER // GROUP_INNER
    b_blk0 = 2 * D_INNER // D_STATE
    c_blk0 = b_blk0 + N_GROUPS
    cb_blk0 = D_INNER // D_STATE
    cc_blk0 = cb_blk0 + N_GROUPS
    proj_b = lambda s: jnp.minimum(s, N_ITEMS - 1) // N_GROUPS
    proj_g = lambda s: jnp.minimum(s, N_ITEMS - 1) % N_GROUPS
    scan_b = lambda s: jnp.maximum(s - 1, 0) // N_GROUPS
    scan_g = lambda s: jnp.maximum(s - 1, 0) % N_GROUPS
    return pl.pallas_call(
        _ssd_kernel,
        grid=(N_ITEMS + 1,),
        in_specs=[
            pl.BlockSpec((1, SEQ, D_MODEL), lambda s: (proj_b(s), 0, 0),
                         pipeline_mode=pl.Buffered(1)),
            pl.BlockSpec((D_MODEL, GROUP_INNER), lambda s: (0, proj_g(s))),
            pl.BlockSpec((D_MODEL, GROUP_INNER), lambda s: (0, x_blk0 + proj_g(s))),
            pl.BlockSpec((D_MODEL, D_STATE), lambda s: (0, b_blk0 + proj_g(s))),
            pl.BlockSpec((D_MODEL, D_STATE), lambda s: (0, c_blk0 + proj_g(s))),
            pl.BlockSpec((D_MODEL, LANES), lambda s: (0, 0)),
            pl.BlockSpec((CONV_WIDTH, GROUP_INNER), lambda s: (0, scan_g(s))),
            pl.BlockSpec((CONV_WIDTH, D_STATE), lambda s: (0, cb_blk0 + scan_g(s))),
            pl.BlockSpec((CONV_WIDTH, D_STATE), lambda s: (0, cc_blk0 + scan_g(s))),
            pl.BlockSpec((1, GROUP_INNER), lambda s: (0, scan_g(s))),
            pl.BlockSpec((1, D_STATE), lambda s: (0, cb_blk0 + scan_g(s))),
            pl.BlockSpec((1, D_STATE), lambda s: (0, cc_blk0 + scan_g(s))),
            pl.BlockSpec((1, LANES), lambda s: (0, 0)),
            pl.BlockSpec((1, LANES), lambda s: (0, 0)),
            pl.BlockSpec((1, GROUP_INNER), lambda s: (0, scan_g(s))),
            pl.BlockSpec((1, GROUP_INNER), lambda s: (0, scan_g(s))),
        ],
        out_specs=[
            pl.BlockSpec((1, SEQ, GROUP_INNER), lambda s: (scan_b(s), 0, scan_g(s))),
            pl.BlockSpec((1, SEQ, 1), lambda s: (scan_b(s), 0, 0)),
        ],
        out_shape=[
            jax.ShapeDtypeStruct((BATCH, SEQ, D_INNER), BF16),
            jax.ShapeDtypeStruct((BATCH, SEQ, 1), F32),
        ],
        scratch_shapes=[
            pltpu.VMEM((SEQ, GROUP_INNER), F32),
            pltpu.VMEM((SEQ, GROUP_INNER), F32),
            pltpu.VMEM((CONV_BLOCKS, SEQ + CONV_PAD, LANES), F32),
            pltpu.VMEM((CONV_BLOCKS, SEQ + CONV_PAD, LANES), F32),
            pltpu.VMEM((D_MODEL, 2 * D_STATE), BF16),
            pltpu.VMEM((2, SEQ // CHUNK, LANES, CHUNK), F32),
            pltpu.VMEM((2, SEQ // CHUNK, LANES, CHUNK), F32),
            pltpu.VMEM((D_STATE, GROUP_INNER), F32),
        ],
        compiler_params=pltpu.CompilerParams(
            dimension_semantics=("arbitrary",),
            vmem_limit_bytes=VMEM_LIMIT),
        name="ssd_scan",
    )(xb, in_w_bf, in_w_bf, in_w_bf, in_w_bf, wdt_bf, cw, cw, cw, cb, cb, cb, dtb, alog, dsk, nw)


def _proj_ln_kernel(u_ref, w_ref, ssq_ref, x_ref, g_ref, b_ref, o_ref):
    rows = o_ref.shape[0] // PROJ_LN_SPLIT
    for h in range(PROJ_LN_SPLIT):
        rs = slice(h * rows, (h + 1) * rows)
        acc = jnp.dot(u_ref[rs, :], w_ref[...], preferred_element_type=F32)
        r = lax.rsqrt(ssq_ref[rs, :] * (1.0 / D_INNER) + RMS_EPS)
        y = ALPHA * x_ref[rs, :] + acc * r
        o_ref[rs, :] = _layer_norm(y, g_ref[...], b_ref[...])


def _proj_ln(u, w_bf, ssq, x, g, b, *, tm=512):
    n_tok = x.shape[0]
    return pl.pallas_call(
        _proj_ln_kernel,
        grid=(n_tok // tm,),
        in_specs=[
            pl.BlockSpec((tm, D_INNER), lambda i: (i, 0)),
            pl.BlockSpec((D_INNER, D_MODEL), lambda i: (0, 0), pipeline_mode=pl.Buffered(1)),
            pl.BlockSpec((tm, 1), lambda i: (i, 0)),
            pl.BlockSpec((tm, D_MODEL), lambda i: (i, 0)),
            pl.BlockSpec((1, D_MODEL), lambda i: (0, 0)),
            pl.BlockSpec((1, D_MODEL), lambda i: (0, 0)),
        ],
        out_specs=pl.BlockSpec((tm, D_MODEL), lambda i: (i, 0)),
        out_shape=jax.ShapeDtypeStruct((n_tok, D_MODEL), F32),
        compiler_params=pltpu.CompilerParams(
            dimension_semantics=("parallel",),
            vmem_limit_bytes=VMEM_LIMIT),
        name="proj_ln",
    )(u, w_bf, ssq, x, g, b)


def _pad_heads(a):
    pad = [(0, 0)] * (a.ndim - 1) + [(0, LANES - N_HEADS)]
    return jnp.pad(a, pad)


def kernel(x, p, pool_w, pool_scale, ssm_in_w, ssm_conv_w, ssm_conv_b, ssm_dt_bias,
           ssm_a_log, ssm_d, ssm_norm_w, ssm_out_w, mlp_w1, mlp_w2, ln_g, ln_b,
           ple_w, ple_gate_w):
    n_tok = BATCH * SEQ
    row = lambda a: a.reshape(1, -1)

    x1 = _pool_ln(x, pool_w[0].astype(BF16), row(pool_scale[0]),
                  row(ln_g[0, 0]), row(ln_b[0, 0]))
    w1_bf, w2_bf = mlp_w1.astype(BF16), mlp_w2.astype(BF16)
    plew_bf, gatew_bf = ple_w.astype(BF16), ple_gate_w.astype(BF16)
    p_tok = p.reshape(DEPTH, n_tok, PLE_DIM)
    x2, x2b = _mlp_ln_ple(0, x1.reshape(n_tok, D_MODEL), w1_bf, w2_bf,
                          row(ln_g[0, 1]), row(ln_b[0, 1]), p_tok, plew_bf, gatew_bf,
                          emit_bf16=True)

    in_w_bf = ssm_in_w[0].astype(BF16)
    wdt = _pad_heads(in_w_bf[:, 2 * D_INNER + 2 * N_GROUPS * D_STATE:])
    yg, ssq = _ssd_scan(
        x2b.reshape(BATCH, SEQ, D_MODEL), in_w_bf, wdt, ssm_conv_w[0], row(ssm_conv_b[0]),
        row(_pad_heads(ssm_dt_bias[0])), row(_pad_heads(ssm_a_log[0])),
        row(jnp.repeat(ssm_d[0], HEAD_DIM)), row(ssm_norm_w[0]))
    x3 = _proj_ln(yg.reshape(n_tok, D_INNER), ssm_out_w[0].astype(BF16),
                  ssq.reshape(n_tok, 1), x2, row(ln_g[1, 0]), row(ln_b[1, 0]))
    x4, _ = _mlp_ln_ple(1, x3, w1_bf, w2_bf,
                        row(ln_g[1, 1]), row(ln_b[1, 1]), p_tok, plew_bf, gatew_bf,
                        emit_bf16=False)
    return x4.reshape(BATCH, SEQ, D_MODEL)
```

```python
import functools

import jax
import jax.numpy as jnp
from jax import lax
from jax.experimental import pallas as pl
from jax.experimental.pallas import tpu as pltpu

F32 = jnp.float32
BF16 = jnp.bfloat16

D_MODEL = 2048
BATCH = 8
SEQ = 2048
DEPTH = 2
ALPHA = (2.0 * DEPTH) ** 0.25
LN_EPS = 1e-5

POOL_WINDOWS = (2, 4, 8, 16)
POOL_GROUP_DIM = D_MODEL // len(POOL_WINDOWS)
POOL_HALO = 16

D_INNER = 2 * D_MODEL
HEAD_DIM = 64
N_HEADS = D_INNER // HEAD_DIM
N_GROUPS = 8
HEADS_PER_GROUP = N_HEADS // N_GROUPS
D_STATE = 128
CONV_WIDTH = 4
CHUNK = 128
GROUP_INNER = D_INNER // N_GROUPS
GROUP_CONV = GROUP_INNER + 2 * D_STATE
RMS_EPS = 1e-5
N_ITEMS = BATCH * N_GROUPS

D_FF = 4 * D_MODEL
PLE_DIM = 256

LANES = 128
SUBLANES = 8
CONV_PAD = SUBLANES
CONV_BLOCKS = GROUP_CONV // LANES
PROJ_COLS = 256
CHUNKS_PER_ITER = 2
PROJ_ROWS = CHUNKS_PER_ITER * CHUNK
FILLS_PER_PIECE = 2
VMEM_LIMIT = 60 * 1024 * 1024


def _silu(x):
    h = 0.5 * x
    return h + h * jnp.tanh(h)


def _layer_norm(y, g, b):
    mu = jnp.mean(y, axis=-1, keepdims=True)
    d = y - mu
    var = jnp.mean(d * d, axis=-1, keepdims=True)
    return d * lax.rsqrt(var + LN_EPS) * g + b


def _pool_ln_kernel(x_ref, halo_ref, w_ref, scale_ref, g_ref, b_ref, o_ref, *, ts):
    i = pl.program_id(1)
    halo_on = (i > 0).astype(F32)
    pos = lax.broadcasted_iota(jnp.int32, (ts, 1), 0) + i * ts
    for gi, win in enumerate(POOL_WINDOWS):
        sl = slice(gi * POOL_GROUP_DIM, (gi + 1) * POOL_GROUP_DIM)
        xg = x_ref[0, :, sl]
        s = jnp.concatenate([halo_ref[0, :, sl] * halo_on, xg], axis=0)
        k = 1
        while k < win:
            s = s + pltpu.roll(s, k, axis=0)
            k *= 2
        cnt = jnp.minimum(pos + 1, win).astype(F32)
        pooled = s[POOL_HALO:] / cnt - xg
        h = jnp.dot(pooled.astype(BF16), w_ref[gi], preferred_element_type=F32)
        o_ref[0, :, sl] = ALPHA * xg + h * scale_ref[:, sl]
    o_ref[0] = _layer_norm(o_ref[0], g_ref[...], b_ref[...])


def _pool_ln(x, w_bf, scale, g, b, *, ts=512):
    nt = SEQ // ts
    hb = ts // POOL_HALO
    return pl.pallas_call(
        functools.partial(_pool_ln_kernel, ts=ts),
        grid=(BATCH, nt),
        in_specs=[
            pl.BlockSpec((1, ts, D_MODEL), lambda b_, i: (b_, i, 0)),
            pl.BlockSpec((1, POOL_HALO, D_MODEL),
                         lambda b_, i: (b_, jnp.maximum(i * hb - 1, 0), 0)),
            pl.BlockSpec((len(POOL_WINDOWS), POOL_GROUP_DIM, POOL_GROUP_DIM),
                         lambda b_, i: (0, 0, 0)),
            pl.BlockSpec((1, D_MODEL), lambda b_, i: (0, 0)),
            pl.BlockSpec((1, D_MODEL), lambda b_, i: (0, 0)),
            pl.BlockSpec((1, D_MODEL), lambda b_, i: (0, 0)),
        ],
        out_specs=pl.BlockSpec((1, ts, D_MODEL), lambda b_, i: (b_, i, 0)),
        out_shape=jax.ShapeDtypeStruct((BATCH, SEQ, D_MODEL), F32),
        compiler_params=pltpu.CompilerParams(
            dimension_semantics=("parallel", "arbitrary"),
            vmem_limit_bytes=VMEM_LIMIT),
        name="pool_ln",
    )(x, x, w_bf, scale, g, b)


def _mlp_kernel(x_ref, w1_ref, w2_ref, g_ref, b_ref, p_ref, plew_ref, gatew_ref,
                o_ref, *rest, n_chunks):
    ob_ref, xb_s = rest if len(rest) == 2 else (None, rest[0])
    j = pl.program_id(1)

    @pl.when(j == 0)
    def _():
        x = x_ref[...]
        xb_s[...] = x.astype(BF16)
        o_ref[...] = ALPHA * x

    h = jnp.dot(xb_s[...], w1_ref[...], preferred_element_type=F32)
    h = jnp.maximum(h, 0.0)
    h = h * h
    o_ref[...] += jnp.dot(h.astype(BF16), w2_ref[...], preferred_element_type=F32)

    @pl.when(j == n_chunks - 1)
    def _():
        x2 = _layer_norm(o_ref[...], g_ref[...], b_ref[...])
        gate = jax.nn.sigmoid(
            jnp.dot(x2.astype(BF16), gatew_ref[...], preferred_element_type=F32))
        pe = jnp.dot(p_ref[...].astype(BF16), plew_ref[...], preferred_element_type=F32)
        out = x2 + gate * pe
        o_ref[...] = out
        if ob_ref is not None:
            ob_ref[...] = out.astype(BF16)


def _mlp_ln_ple(layer, x, w1_bf, w2_bf, g, b, p, plew_bf, gatew_bf, *, emit_bf16,
                tm=512, tf=1024):
    n_tok = x.shape[0]
    n_chunks = D_FF // tf
    out_specs = [pl.BlockSpec((tm, D_MODEL), lambda i, j: (i, 0))]
    out_shape = [jax.ShapeDtypeStruct((n_tok, D_MODEL), F32)]
    if emit_bf16:
        out_specs.append(pl.BlockSpec((tm, D_MODEL), lambda i, j: (i, 0)))
        out_shape.append(jax.ShapeDtypeStruct((n_tok, D_MODEL), BF16))
    outs = pl.pallas_call(
        functools.partial(_mlp_kernel, n_chunks=n_chunks),
        grid=(n_tok // tm, n_chunks),
        in_specs=[
            pl.BlockSpec((tm, D_MODEL), lambda i, j: (i, 0)),
            pl.BlockSpec((None, D_MODEL, tf), lambda i, j: (layer, 0, j)),
            pl.BlockSpec((None, tf, D_MODEL), lambda i, j: (layer, j, 0)),
            pl.BlockSpec((1, D_MODEL), lambda i, j: (0, 0)),
            pl.BlockSpec((1, D_MODEL), lambda i, j: (0, 0)),
            pl.BlockSpec((None, tm, PLE_DIM), lambda i, j: (layer, i, 0)),
            pl.BlockSpec((None, PLE_DIM, D_MODEL), lambda i, j: (layer, 0, 0),
                         pipeline_mode=pl.Buffered(1)),
            pl.BlockSpec((None, D_MODEL, D_MODEL), lambda i, j: (layer, 0, 0),
                         pipeline_mode=pl.Buffered(1)),
        ],
        out_specs=out_specs,
        out_shape=out_shape,
        scratch_shapes=[pltpu.VMEM((tm, D_MODEL), BF16)],
        compiler_params=pltpu.CompilerParams(
            dimension_semantics=("parallel", "arbitrary"),
            vmem_limit_bytes=VMEM_LIMIT),
        name="mlp_ln_ple",
    )(x, w1_bf, w2_bf, g, b, p, plew_bf, gatew_bf)
    return (outs[0], outs[1]) if emit_bf16 else (outs[0], None)


def _ssd_kernel(xb_ref, wz_ref, wx_ref, wb_ref, wc_ref, wdt_ref, cwx_ref, cwb_ref, cwc_ref,
                cbx_ref, cbb_ref, cbc_ref, dtb_ref, alog_ref, dsk_ref, nw_ref,
                yg_ref, ssq_ref, z0_s, z1_s, xbc0_s, xbc1_s, wbc_s, dtt_s, acst_s, state_s):
    s = pl.program_id(0)
    t = jnp.minimum(s, N_ITEMS - 1)
    t_bslot = (t // N_GROUPS) & 1
    sc = jnp.maximum(s - 1, 0)
    g = sc % N_GROUPS
    s_bslot = (sc // N_GROUPS) & 1
    slots = ((z0_s, xbc0_s), (z1_s, xbc1_s))

    @pl.when((t % N_GROUPS == 0) & (s < N_ITEMS))
    def _():
        dt_raw = jnp.dot(xb_ref[0], wdt_ref[...], preferred_element_type=F32) + dtb_ref[...]
        dt = jnp.maximum(dt_raw, 0.0) + jnp.log1p(jnp.exp(-jnp.abs(dt_raw)))
        a = dt * (-jnp.exp(alog_ref[...]))
        in_chunk = lax.broadcasted_iota(jnp.int32, (SEQ, 1), 0) & (CHUNK - 1)
        k = 1
        while k < CHUNK:
            a = a + jnp.where(in_chunk >= k, pltpu.roll(a, k, axis=0), 0.0)
            k *= 2
        for c in range(SEQ // CHUNK):
            dtt_s[t_bslot, c] = dt[c * CHUNK:(c + 1) * CHUNK, :].T
            acst_s[t_bslot, c] = a[c * CHUNK:(c + 1) * CHUNK, :].T

    @pl.when(s < N_ITEMS)
    def _():
        wbc_s[:, :D_STATE] = wb_ref[...]
        wbc_s[:, D_STATE:] = wc_ref[...]

    @pl.when(s == 0)
    def _():
        for _, xbc_s in slots:
            for j in range(CONV_BLOCKS):
                xbc_s[j, 0:CONV_PAD, :] = jnp.zeros((CONV_PAD, LANES), F32)

    def proj_pieces(c, z_s, xbc_s):
        r0 = pl.multiple_of(c * PROJ_ROWS, PROJ_ROWS)
        xr = xb_ref[0, pl.ds(r0, PROJ_ROWS), :]

        def z_piece(n):
            cols = slice(n * PROJ_COLS, (n + 1) * PROJ_COLS)
            z_s[pl.ds(r0, PROJ_ROWS), cols] = jnp.dot(
                xr, wz_ref[:, cols], preferred_element_type=F32)

        def xbc_piece(n):
            if n < GROUP_INNER // PROJ_COLS:
                w = wx_ref[:, n * PROJ_COLS:(n + 1) * PROJ_COLS]
            else:
                w = wbc_s[...]
            res = jnp.dot(xr, w, preferred_element_type=F32)
            for jj in range(PROJ_COLS // LANES):
                j = n * (PROJ_COLS // LANES) + jj
                xbc_s[j, pl.ds(r0 + CONV_PAD, PROJ_ROWS), :] = res[:, jj * LANES:(jj + 1) * LANES]

        return ([functools.partial(z_piece, n) for n in range(GROUP_INNER // PROJ_COLS)]
                + [functools.partial(xbc_piece, n) for n in range(GROUP_CONV // PROJ_COLS)])

    lo = lax.broadcasted_iota(jnp.int32, (1, LANES), 1) < HEAD_DIM
    ri = lax.broadcasted_iota(jnp.int32, (CHUNK, CHUNK), 0)
    ci = lax.broadcasted_iota(jnp.int32, (CHUNK, CHUNK), 1)
    causal = ri >= ci


    def conv_chunk(c, xbc_s, fill):
        cw = jnp.concatenate([cwx_ref[...], cwb_ref[...], cwc_ref[...]], axis=1)
        cb = jnp.concatenate([cbx_ref[...], cbb_ref[...], cbc_ref[...]], axis=1)
        r0 = pl.multiple_of(c * CHUNK, CHUNK)
        u = []
        for j in range(CONV_BLOCKS):
            bl = slice(j * LANES, (j + 1) * LANES)
            acc = cb[:, bl]
            for k in range(CONV_WIDTH):
                off = CONV_PAD - (CONV_WIDTH - 1) + k
                acc = acc + xbc_s[j, pl.ds(r0 + off, CHUNK), :] * cw[k:k + 1, bl]
            u.append(_silu(acc))
            if j % 2 == 1:
                fill()
        return u

    def head_pair_columns(rows):
        stacked = jnp.concatenate(
            [jnp.broadcast_to(r, (HEAD_DIM, CHUNK)) for r in rows], axis=0)
        return stacked.T

    def scan_chunk(c, u, z_s, fill):
        dsk = dsk_ref[...]
        nw = nw_ref[...]
        r0 = pl.multiple_of(c * CHUNK, CHUNK)
        rows = pl.ds(r0, CHUNK)
        bm = u[CONV_BLOCKS - 2]
        cm = u[CONV_BLOCKS - 1]
        bm_bf = bm.astype(BF16)
        cm_bf = cm.astype(BF16)
        cbm = lax.dot_general(cm_bf, bm_bf, (((1,), (1,)), ((), ())),
                              preferred_element_type=F32)
        state = state_s[...]
        y_off = jnp.dot(cm_bf, state.astype(BF16), preferred_element_type=F32)

        prep = []
        for kb in range(HEADS_PER_GROUP // 2):
            heads = [g * HEADS_PER_GROUP + 2 * kb + hh for hh in range(2)]
            a_rows = [acst_s[s_bslot, c, pl.ds(h, 1), :] for h in heads]
            d_rows = [dtt_s[s_bslot, c, pl.ds(h, 1), :] for h in heads]
            a_blk = head_pair_columns(a_rows)
            dt_blk = head_pair_columns(d_rows)
            a_end = a_blk[CHUNK - 1:CHUNK, :]
            xdt = u[kb] * dt_blk
            lmats = []
            for a_row in a_rows:
                a_col = jnp.broadcast_to(a_row, (CHUNK, CHUNK)).T
                lmats.append(jnp.exp(jnp.where(causal, a_col - a_row, -jnp.inf)))
            rhs = jnp.concatenate([jnp.where(lo, xdt, 0.0), jnp.where(lo, 0.0, xdt)],
                                  axis=0).astype(BF16)
            prep.append((lmats, rhs, jnp.exp(a_blk), xdt * jnp.exp(a_end - a_blk),
                         jnp.exp(a_end)))
            if kb % 2 == 1:
                fill()
        xdec = jnp.concatenate([p[3] for p in prep], axis=1).astype(BF16)
        cdec = jnp.concatenate([p[4] for p in prep], axis=1)
        st_new = jnp.dot(bm.T.astype(BF16), xdec, preferred_element_type=F32)
        state_s[...] = state * cdec + st_new

        y_blocks = []
        for kb, (lmats, rhs, decay, _, _) in enumerate(prep):
            bl = slice(kb * LANES, (kb + 1) * LANES)
            lhs = jnp.concatenate([(cbm * lm).astype(BF16) for lm in lmats], axis=1)
            y_diag = jnp.dot(lhs, rhs, preferred_element_type=F32)
            y_blocks.append(y_diag + y_off[:, bl] * decay + u[kb] * dsk[:, bl])
        y = jnp.concatenate(y_blocks, axis=1)

        zc = z_s[rows, :]
        v = y * _silu(zc)
        ssq_ref[0, rows, :] += jnp.sum(v * v, axis=1, keepdims=True)
        yg_ref[0, rows, :] = (v * nw).astype(BF16)

    def run(scan_slot, proj_slot):
        def body(c, carry):
            pieces = iter(proj_pieces(c, *slots[proj_slot]) if proj_slot is not None else ())

            calls = [0]

            def fill():
                calls[0] += 1
                if calls[0] % FILLS_PER_PIECE == 0:
                    piece = next(pieces, None)
                    if piece is not None:
                        piece()

            if scan_slot is not None:
                z_s, xbc_s = slots[scan_slot]
                chunks = [c * CHUNKS_PER_ITER + cc for cc in range(CHUNKS_PER_ITER)]
                us = [conv_chunk(ch, xbc_s, fill) for ch in chunks]
                for ch, u in zip(chunks, us):
                    scan_chunk(ch, u, z_s, fill)
            for piece in pieces:
                piece()
            return carry
        lax.fori_loop(0, SEQ // PROJ_ROWS, body, 0)

    @pl.when(s == 0)
    def _():
        run(None, 0)

    @pl.when(s > 0)
    def _():
        state_s[...] = jnp.zeros_like(state_s)

        @pl.when(g == 0)
        def _():
            ssq_ref[...] = jnp.zeros_like(ssq_ref)

    for parity in range(2):
        @pl.when((s > 0) & (s < N_ITEMS) & (s % 2 == parity))
        def _():
            run(1 - parity, parity)

    @pl.when(s == N_ITEMS)
    def _():
        run((N_ITEMS - 1) % 2, None)


def _ssd_scan(xb, in_w_bf, wdt_bf, cw, cb, dtb, alog, dsk, nw):
    x_blk0 = D_INNER // GROUP_INNER
    b_blk0 = 2 * D_INNER // D_STATE
    c_blk0 = b_blk0 + N_GROUPS
    cb_blk0 = D_INNER // D_STATE
    cc_blk0 = cb_blk0 + N_GROUPS
    proj_b = lambda s: jnp.minimum(s, N_ITEMS - 1) // N_GROUPS
    proj_g = lambda s: jnp.minimum(s, N_ITEMS - 1) % N_GROUPS
    scan_b = lambda s: jnp.maximum(s - 1, 0) // N_GROUPS
    scan_g = lambda s: jnp.maximum(s - 1, 0) % N_GROUPS
    return pl.pallas_call(
        _ssd_kernel,
        grid=(N_ITEMS + 1,),
        in_specs=[
            pl.BlockSpec((1, SEQ, D_MODEL), lambda s: (proj_b(s), 0, 0),
                         pipeline_mode=pl.Buffered(1)),
            pl.BlockSpec((D_MODEL, GROUP_INNER), lambda s: (0, proj_g(s))),
            pl.BlockSpec((D_MODEL, GROUP_INNER), lambda s: (0, x_blk0 + proj_g(s))),
            pl.BlockSpec((D_MODEL, D_STATE), lambda s: (0, b_blk0 + proj_g(s))),
            pl.BlockSpec((D_MODEL, D_STATE), lambda s: (0, c_blk0 + proj_g(s))),
            pl.BlockSpec((D_MODEL, LANES), lambda s: (0, 0)),
            pl.BlockSpec((CONV_WIDTH, GROUP_INNER), lambda s: (0, scan_g(s))),
            pl.BlockSpec((CONV_WIDTH, D_STATE), lambda s: (0, cb_blk0 + scan_g(s))),
            pl.BlockSpec((CONV_WIDTH, D_STATE), lambda s: (0, cc_blk0 + scan_g(s))),
            pl.BlockSpec((1, GROUP_INNER), lambda s: (0, scan_g(s))),
            pl.BlockSpec((1, D_STATE), lambda s: (0, cb_blk0 + scan_g(s))),
            pl.BlockSpec((1, D_STATE), lambda s: (0, cc_blk0 + scan_g(s))),
            pl.BlockSpec((1, LANES), lambda s: (0, 0)),
            pl.BlockSpec((1, LANES), lambda s: (0, 0)),
            pl.BlockSpec((1, GROUP_INNER), lambda s: (0, scan_g(s))),
            pl.BlockSpec((1, GROUP_INNER), lambda s: (0, scan_g(s))),
        ],
        out_specs=[
            pl.BlockSpec((1, SEQ, GROUP_INNER), lambda s: (scan_b(s), 0, scan_g(s))),
            pl.BlockSpec((1, SEQ, 1), lambda s: (scan_b(s), 0, 0)),
        ],
        out_shape=[
            jax.ShapeDtypeStruct((BATCH, SEQ, D_INNER), BF16),
            jax.ShapeDtypeStruct((BATCH, SEQ, 1), F32),
        ],
        scratch_shapes=[
            pltpu.VMEM((SEQ, GROUP_INNER), F32),
            pltpu.VMEM((SEQ, GROUP_INNER), F32),
            pltpu.VMEM((CONV_BLOCKS, SEQ + CONV_PAD, LANES), F32),
            pltpu.VMEM((CONV_BLOCKS, SEQ + CONV_PAD, LANES), F32),
            pltpu.VMEM((D_MODEL, 2 * D_STATE), BF16),
            pltpu.VMEM((2, SEQ // CHUNK, LANES, CHUNK), F32),
            pltpu.VMEM((2, SEQ // CHUNK, LANES, CHUNK), F32),
            pltpu.VMEM((D_STATE, GROUP_INNER), F32),
        ],
        compiler_params=pltpu.CompilerParams(
            dimension_semantics=("arbitrary",),
            vmem_limit_bytes=VMEM_LIMIT),
        name="ssd_scan",
    )(xb, in_w_bf, in_w_bf, in_w_bf, in_w_bf, wdt_bf, cw, cw, cw, cb, cb, cb, dtb, alog, dsk, nw)


def _proj_ln_kernel(u_ref, w_ref, ssq_ref, x_ref, g_ref, b_ref, o_ref):
    acc = jnp.dot(u_ref[...], w_ref[...], preferred_element_type=F32)
    r = lax.rsqrt(ssq_ref[...] * (1.0 / D_INNER) + RMS_EPS)
    y = ALPHA * x_ref[...] + acc * r
    o_ref[...] = _layer_norm(y, g_ref[...], b_ref[...])


def _proj_ln(u, w_bf, ssq, x, g, b, *, tm=512):
    n_tok = x.shape[0]
    return pl.pallas_call(
        _proj_ln_kernel,
        grid=(n_tok // tm,),
        in_specs=[
            pl.BlockSpec((tm, D_INNER), lambda i: (i, 0)),
            pl.BlockSpec((D_INNER, D_MODEL), lambda i: (0, 0), pipeline_mode=pl.Buffered(1)),
            pl.BlockSpec((tm, 1), lambda i: (i, 0)),
            pl.BlockSpec((tm, D_MODEL), lambda i: (i, 0)),
            pl.BlockSpec((1, D_MODEL), lambda i: (0, 0)),
            pl.BlockSpec((1, D_MODEL), lambda i: (0, 0)),
        ],
        out_specs=pl.BlockSpec((tm, D_MODEL), lambda i: (i, 0)),
        out_shape=jax.ShapeDtypeStruct((n_tok, D_MODEL), F32),
        compiler_params=pltpu.CompilerParams(
            dimension_semantics=("parallel",),
            vmem_limit_bytes=VMEM_LIMIT),
        name="proj_ln",
    )(u, w_bf, ssq, x, g, b)


def _pad_heads(a):
    pad = [(0, 0)] * (a.ndim - 1) + [(0, LANES - N_HEADS)]
    return jnp.pad(a, pad)


def kernel(x, p, pool_w, pool_scale, ssm_in_w, ssm_conv_w, ssm_conv_b, ssm_dt_bias,
           ssm_a_log, ssm_d, ssm_norm_w, ssm_out_w, mlp_w1, mlp_w2, ln_g, ln_b,
           ple_w, ple_gate_w):
    n_tok = BATCH * SEQ
    row = lambda a: a.reshape(1, -1)

    x1 = _pool_ln(x, pool_w[0].astype(BF16), row(pool_scale[0]),
                  row(ln_g[0, 0]), row(ln_b[0, 0]))
    w1_bf, w2_bf = mlp_w1.astype(BF16), mlp_w2.astype(BF16)
    plew_bf, gatew_bf = ple_w.astype(BF16), ple_gate_w.astype(BF16)
    p_tok = p.reshape(DEPTH, n_tok, PLE_DIM)
    x2, x2b = _mlp_ln_ple(0, x1.reshape(n_tok, D_MODEL), w1_bf, w2_bf,
                          row(ln_g[0, 1]), row(ln_b[0, 1]), p_tok, plew_bf, gatew_bf,
                          emit_bf16=True)

    in_w_bf = ssm_in_w[0].astype(BF16)
    wdt = _pad_heads(in_w_bf[:, 2 * D_INNER + 2 * N_GROUPS * D_STATE:])
    yg, ssq = _ssd_scan(
        x2b.reshape(BATCH, SEQ, D_MODEL), in_w_bf, wdt, ssm_conv_w[0], row(ssm_conv_b[0]),
        row(_pad_heads(ssm_dt_bias[0])), row(_pad_heads(ssm_a_log[0])),
        row(jnp.repeat(ssm_d[0], HEAD_DIM)), row(ssm_norm_w[0]))
    x3 = _proj_ln(yg.reshape(n_tok, D_INNER), ssm_out_w[0].astype(BF16),
                  ssq.reshape(n_tok, 1), x2, row(ln_g[1, 0]), row(ln_b[1, 0]))
    x4, _ = _mlp_ln_ple(1, x3, w1_bf, w2_bf,
                        row(ln_g[1, 1]), row(ln_b[1, 1]), p_tok, plew_bf, gatew_bf,
                        emit_bf16=False)
    return x4.reshape(BATCH, SEQ, D_MODEL)
```
